```python
import jax, jax.numpy as jnp
from jax import lax
import numpy as np

D_MODEL = 1024
BATCH = 4
SEQ = 4096
DEPTH = 1
DEC_BATCH = 128
DEC_SEQ = 1
PAST_LEN = 8192
PAGE_SIZE = 128

N_META = 16
Q_BLOCK = 128
EPS = 1e-6
NEG_INF = -1e30
SB_HEADS = 8
SB_HEAD_DIM = 64
SB_WIDTH = SB_HEADS * SB_HEAD_DIM
SB_SCALE = SB_HEAD_DIM ** -0.5
MLA_HEADS = 8
MLA_Q_RANK = 384
MLA_KV_RANK = 256
MLA_NOPE_DIM = 64
MLA_ROPE_DIM = 32
MLA_V_DIM = 64
MLA_SCALE = (MLA_NOPE_DIM + MLA_ROPE_DIM) ** -0.5
ROPE_THETA = 10000.0
D_FF = -(-8 * D_MODEL // (3 * 256)) * 256
IN_SPLITS = (SB_WIDTH, SB_WIDTH, SB_WIDTH, MLA_Q_RANK, MLA_KV_RANK, MLA_ROPE_DIM, D_MODEL, D_MODEL)
IN_WIDTH = sum(IN_SPLITS)
SPLIT_POINTS = tuple(int(v) for v in np.cumsum(IN_SPLITS)[:-1])

kernel_name = 'hybrid_stickbreak_mla_decoder_step'


def rms_norm(x, g):
    xf = x.astype(jnp.float32)
    y = xf * lax.rsqrt(jnp.mean(xf * xf, axis=-1, keepdims=True) + EPS)
    return (y * g.astype(jnp.float32)).astype(x.dtype)


def apply_rope(x, pos):
    half = x.shape[-1] // 2
    inv = ROPE_THETA ** (-jnp.arange(half, dtype=jnp.float32) / half)
    ang = pos.astype(jnp.float32)[:, None] * inv[None, :]
    cos = jnp.cos(ang)[None, :, None, :]
    sin = jnp.sin(ang)[None, :, None, :]
    xf = x.astype(jnp.float32)
    x1, x2 = xf[..., :half], xf[..., half:]
    return jnp.concatenate([x1 * cos - x2 * sin, x2 * cos + x1 * sin], axis=-1).astype(x.dtype)


def project_mixers(h, pos, w_in, q_a_norm, w_uq, q_nope_norm, q_rope_norm, kv_a_norm, k_rope_norm):
    b, t, _ = h.shape
    sq, sk, sv, dq, dkv, kr, g_sb, g_mla = jnp.split(h @ w_in, SPLIT_POINTS, axis=-1)
    hd = (b, t, SB_HEADS, SB_HEAD_DIM)
    q = (rms_norm(dq, q_a_norm) @ w_uq).reshape(b, t, MLA_HEADS, MLA_NOPE_DIM + MLA_ROPE_DIM)
    q_nope = rms_norm(q[..., :MLA_NOPE_DIM], q_nope_norm)
    q_rope = apply_rope(rms_norm(q[..., MLA_NOPE_DIM:], q_rope_norm), pos)
    c_kv = rms_norm(dkv, kv_a_norm)
    k_rope = apply_rope(rms_norm(kr, k_rope_norm)[:, :, None, :], pos)[:, :, 0, :]
    return (sq.reshape(hd), sk.reshape(hd), sv.reshape(hd), q_nope, q_rope, c_kv, k_rope,
            jax.nn.sigmoid(g_sb), jax.nn.sigmoid(g_mla))


def mla_expand(c_kv, w_ukv, k_nope_norm):
    b, t, _ = c_kv.shape
    kv = (c_kv @ w_ukv).reshape(b, t, MLA_HEADS, MLA_NOPE_DIM + MLA_V_DIM)
    return rms_norm(kv[..., :MLA_NOPE_DIM], k_nope_norm), kv[..., MLA_NOPE_DIM:]


def sb_attend(q, k, v, q_pos, k_pos):
    z = jnp.einsum('bqhd,bkhd->bhqk', q, k, preferred_element_type=jnp.float32) * SB_SCALE
    valid = k_pos[None, :] < q_pos[:, None]
    log_rem = jnp.where(valid, jax.nn.log_sigmoid(-z), 0.0)
    later = lax.cumsum(log_rem, axis=3, reverse=True) - log_rem
    w = jnp.where(valid, jnp.exp(jax.nn.log_sigmoid(z) + later), 0.0)
    out = jnp.einsum('bhqk,bkhd->bqhd', w.astype(v.dtype), v, preferred_element_type=jnp.float32)
    return out.astype(v.dtype)


def mla_attend(q_nope, q_rope, k_nope, k_rope, v, q_pos, k_pos):
    s = (jnp.einsum('bqhd,bkhd->bhqk', q_nope, k_nope, preferred_element_type=jnp.float32)
         + jnp.einsum('bqhr,bkr->bhqk', q_rope, k_rope, preferred_element_type=jnp.float32)) * MLA_SCALE
    s = jnp.where(k_pos[None, :] <= q_pos[:, None], s, NEG_INF)
    p = jax.nn.softmax(s, axis=-1)
    out = jnp.einsum('bhqk,bkhd->bqhd', p.astype(v.dtype), v, preferred_element_type=jnp.float32)
    return out.astype(v.dtype)


def sweep_query_blocks(attend, q_parts, q_pos):
    meta = attend(tuple(q[:, :N_META] for q in q_parts), q_pos[:N_META])
    real = tuple(q[:, N_META:] for q in q_parts)
    b, t = real[0].shape[:2]
    nb = t // Q_BLOCK
    blocks = tuple(jnp.swapaxes(q.reshape((b, nb, Q_BLOCK) + q.shape[2:]), 0, 1) for q in real)
    pos_blocks = q_pos[N_META:].reshape(nb, Q_BLOCK)
    out = lax.map(lambda a: attend(a[0], a[1]), (blocks, pos_blocks))
    out = jnp.swapaxes(out, 0, 1).reshape((b, t) + out.shape[3:])
    return jnp.concatenate([meta, out], axis=1)


def merge_and_ffn(x, o_sb, o_mla, g_sb, g_mla, w_sb_o, w_mla_o, w_out, ffn_norm, w_gu, w_down):
    b, t, _ = x.shape
    y = g_sb * (o_sb.reshape(b, t, -1) @ w_sb_o) + g_mla * (o_mla.reshape(b, t, -1) @ w_mla_o)
    x = x + y @ w_out
    h = rms_norm(x, ffn_norm)
    gate, up = jnp.split(h @ w_gu, 2, axis=-1)
    return x + (jax.nn.silu(gate) * up) @ w_down


def gather_pages(pool, page_table):
    g = pool[page_table]
    return g.reshape((g.shape[0], g.shape[1] * g.shape[2]) + g.shape[3:])


def setup_inputs(seed: int = 0) -> dict:
    key = jax.random.key(seed)
    ks = iter(jax.random.split(key, 32))
    f32 = jnp.float32
    n_pages = PAST_LEN // PAGE_SIZE
    n_used = DEC_BATCH * n_pages
    n_pool = n_used + n_used // 4

    def normal(shape, scale=1.0):
        return jax.random.normal(next(ks), shape, f32) * scale

    def gain(n):
        return 1.0 + 0.02 * normal((DEPTH, n))

    x_prompt = normal((BATCH, SEQ, D_MODEL))
    x_sample = normal((DEC_BATCH, DEC_SEQ, D_MODEL))
    cache_sb_k = normal((DEPTH, n_pool, PAGE_SIZE, SB_HEADS, SB_HEAD_DIM))
    cache_sb_v = normal((DEPTH, n_pool, PAGE_SIZE, SB_HEADS, SB_HEAD_DIM))
    cache_mla_ckv = normal((DEPTH, n_pool, PAGE_SIZE, MLA_KV_RANK))
    cache_mla_krope = normal((DEPTH, n_pool, PAGE_SIZE, MLA_ROPE_DIM))
    page_table = jax.random.permutation(next(ks), n_pool)[:n_used].reshape(DEC_BATCH, n_pages).astype(jnp.int32)
    return {
        'x_prompt': x_prompt,
        'x_sample': x_sample,
        'cache_sb_k': cache_sb_k,
        'cache_sb_v': cache_sb_v,
        'cache_mla_ckv': cache_mla_ckv,
        'cache_mla_krope': cache_mla_krope,
        'page_table': page_table,
        'meta_tokens': normal((N_META, D_MODEL)),
        'attn_norm': gain(D_MODEL),
        'w_in': normal((DEPTH, D_MODEL, IN_WIDTH), D_MODEL ** -0.5),
        'q_a_norm': gain(MLA_Q_RANK),
        'w_uq': normal((DEPTH, MLA_Q_RANK, MLA_HEADS * (MLA_NOPE_DIM + MLA_ROPE_DIM)), MLA_Q_RANK ** -0.5),
        'q_nope_norm': gain(MLA_NOPE_DIM),
        'q_rope_norm': gain(MLA_ROPE_DIM),
        'kv_a_norm': gain(MLA_KV_RANK),
        'k_rope_norm': gain(MLA_ROPE_DIM),
        'w_ukv': normal((DEPTH, MLA_KV_RANK, MLA_HEADS * (MLA_NOPE_DIM + MLA_V_DIM)), MLA_KV_RANK ** -0.5),
        'k_nope_norm': gain(MLA_NOPE_DIM),
        'w_sb_o': normal((DEPTH, SB_WIDTH, D_MODEL), SB_WIDTH ** -0.5),
        'w_mla_o': normal((DEPTH, MLA_HEADS * MLA_V_DIM, D_MODEL), (MLA_HEADS * MLA_V_DIM) ** -0.5),
        'w_out': normal((DEPTH, D_MODEL, D_MODEL), D_MODEL ** -0.5),
        'ffn_norm': gain(D_MODEL),
        'w_gu': normal((DEPTH, D_MODEL, 2 * D_FF), D_MODEL ** -0.5),
        'w_down': normal((DEPTH, D_FF, D_MODEL), D_FF ** -0.5),
    }


def reference(x_prompt, x_sample, cache_sb_k, cache_sb_v, cache_mla_ckv, cache_mla_krope, page_table,
              meta_tokens, attn_norm, w_in, q_a_norm, w_uq, q_nope_norm, q_rope_norm, kv_a_norm,
              k_rope_norm, w_ukv, k_nope_norm, w_sb_o, w_mla_o, w_out, ffn_norm, w_gu, w_down):
    b = x_prompt.shape[0]
    meta = jnp.broadcast_to(meta_tokens.astype(x_prompt.dtype)[None], (b, N_META, x_prompt.shape[-1]))
    xp = jnp.concatenate([meta, x_prompt], axis=1)
    pos_p = jnp.arange(xp.shape[1])
    past_len = page_table.shape[1] * PAGE_SIZE
    xs = x_sample
    pos_s = past_len + jnp.arange(xs.shape[1])
    kpos_s = jnp.arange(past_len + xs.shape[1])
    new_p = ([], [], [], [])
    new_s = ([], [], [], [])
    for l in range(DEPTH):
        proj_w = (w_in[l], q_a_norm[l], w_uq[l], q_nope_norm[l], q_rope_norm[l], kv_a_norm[l], k_rope_norm[l])
        out_w = (w_sb_o[l], w_mla_o[l], w_out[l], ffn_norm[l], w_gu[l], w_down[l])

        sq, sk, sv, qn, qr, ckv, kr, g_sb, g_mla = project_mixers(rms_norm(xp, attn_norm[l]), pos_p, *proj_w)
        kn, mv = mla_expand(ckv, w_ukv[l], k_nope_norm[l])
        o_sb = sweep_query_blocks(lambda qs, qp: sb_attend(qs[0], sk, sv, qp, pos_p), (sq,), pos_p)
        o_mla = sweep_query_blocks(lambda qs, qp: mla_attend(qs[0], qs[1], kn, kr, mv, qp, pos_p), (qn, qr), pos_p)
        xp = merge_and_ffn(xp, o_sb, o_mla, g_sb, g_mla, *out_w)
        for lst, row in zip(new_p, (sk, sv, ckv, kr)):
            lst.append(row)

        sq, sk, sv, qn, qr, ckv, kr, g_sb, g_mla = project_mixers(rms_norm(xs, attn_norm[l]), pos_s, *proj_w)
        k_all = jnp.concatenate([gather_pages(cache_sb_k[l], page_table), sk], axis=1)
        v_all = jnp.concatenate([gather_pages(cache_sb_v[l], page_table), sv], axis=1)
        ckv_all = jnp.concatenate([gather_pages(cache_mla_ckv[l], page_table), ckv], axis=1)
        kr_all = jnp.concatenate([gather_pages(cache_mla_krope[l], page_table), kr], axis=1)
        kn_all, mv_all = mla_expand(ckv_all, w_ukv[l], k_nope_norm[l])
        o_sb = sb_attend(sq, k_all, v_all, pos_s, kpos_s)
        o_mla = mla_attend(qn, qr, kn_all, kr_all, mv_all, pos_s, kpos_s)
        xs = merge_and_ffn(xs, o_sb, o_mla, g_sb, g_mla, *out_w)
        for lst, row in zip(new_s, (sk, sv, ckv, kr)):
            lst.append(row)

    return (xp[:, N_META:], xs,
            jnp.stack(new_p[0]), jnp.stack(new_p[1]), jnp.stack(new_p[2]), jnp.stack(new_p[3]),
            jnp.stack(new_s[0]), jnp.stack(new_s[1]), jnp.stack(new_s[2]), jnp.stack(new_s[3]))
```

```python
import functools

import numpy as np
import jax
import jax.numpy as jnp
from jax import lax
from jax.experimental import pallas as pl
from jax.experimental.pallas import tpu as pltpu

F32 = jnp.float32
BF16 = jnp.bfloat16

D_MODEL = 1024
N_META = 16
EPS = 1e-6
NEG_INF = -1e30
HEADS = 8
HEAD_DIM = 64
SB_WIDTH = HEADS * HEAD_DIM
SB_SCALE = HEAD_DIM ** -0.5
Q_RANK = 384
KV_RANK = 256
NOPE = 64
ROPE = 32
V_DIM = 64
MLA_SCALE = (NOPE + ROPE) ** -0.5
ROPE_THETA = 10000.0
D_FF = 2816
PAGE = 128
IN_SPLITS = (SB_WIDTH, SB_WIDTH, SB_WIDTH, Q_RANK, KV_RANK, ROPE, D_MODEL, D_MODEL)
SPLIT_POINTS = tuple(int(v) for v in np.cumsum(IN_SPLITS)[:-1])

LANE = 128
HEAD_SLAB = LANE
MLA_WIDTH = HEADS * HEAD_SLAB
C_SQ, C_SK, C_SV = 0, 512, 1024
C_DQ, C_DKV = 1536, 1920
C_GSB, C_GML, C_KR = 2176, 3200, 4224
IN_WIDTH_P = 4352
VMEM_LIMIT = 56 * 1024 * 1024

NT_DIMS = (((1,), (1,)), ((), ()))


def _dot(a, b):
    return jnp.dot(a, b, preferred_element_type=F32)


def _dot_nt(a, b):
    return lax.dot_general(a, b, NT_DIMS, preferred_element_type=F32)


def _rms(x, g):
    return x * lax.rsqrt(jnp.mean(x * x, axis=-1, keepdims=True) + EPS) * g


def _prep_w_in(w):
    sq, sk, sv, dq, dkv, kr, gsb, gml = jnp.split(w, SPLIT_POINTS, axis=1)
    krg = jnp.concatenate([kr, jnp.roll(kr, 16, axis=1), jnp.zeros((D_MODEL, 64), F32)], axis=1)
    return jnp.concatenate([sq * SB_SCALE, sk, sv, dq, dkv, gsb, gml, krg], axis=1).astype(BF16)


def _prep_w_uq(w):
    w = w.reshape(Q_RANK, HEADS, NOPE + ROPE)
    nope, rope = w[..., :NOPE], w[..., NOPE:]
    z32 = jnp.zeros((Q_RANK, HEADS, 32), F32)
    z64 = jnp.zeros((Q_RANK, HEADS, 64), F32)
    main = jnp.concatenate([nope, rope, z32], axis=-1).reshape(Q_RANK, MLA_WIDTH)
    swap = jnp.concatenate([z64, jnp.roll(rope, 16, axis=-1), z32], axis=-1).reshape(Q_RANK, MLA_WIDTH)
    return jnp.concatenate([main, swap], axis=1).astype(BF16)


def _prep_w_ukv(w):
    w = w.reshape(KV_RANK, HEADS, NOPE + V_DIM)
    wk, wv = w[..., :NOPE], w[..., NOPE:]
    z64 = jnp.zeros((KV_RANK, HEADS, 64), F32)
    k_lay = jnp.concatenate([wk, z64], axis=-1).reshape(KV_RANK, MLA_WIDTH)
    lo = jnp.concatenate([wv, z64], axis=-1)
    hi = jnp.concatenate([z64, wv], axis=-1)
    odd = (jnp.arange(HEADS) % 2 == 1)[None, :, None]
    v_lay = jnp.where(odd, hi, lo).reshape(KV_RANK, MLA_WIDTH)
    w_proj = jnp.concatenate([k_lay, v_lay], axis=1).astype(BF16)
    wuk_t = wk.reshape(KV_RANK, HEADS * NOPE).T.astype(BF16)
    wuk_t_pad = k_lay.T.astype(BF16)
    wuv = wv.reshape(KV_RANK, HEADS * V_DIM).astype(BF16)
    return w_proj, wuk_t, wuk_t_pad, wuv


def _seg_matrix():
    m = np.zeros((LANE, LANE), np.float32)
    m[:NOPE, :NOPE] = 1.0 / NOPE
    m[NOPE:NOPE + ROPE, NOPE:NOPE + ROPE] = 1.0 / ROPE
    z = np.zeros_like(m)
    return jnp.asarray(np.block([[m, z], [z, m]]), BF16)


def _cumsum_matrix():
    j = np.arange(LANE)
    later = (j[:, None] > j[None, :]).astype(np.float32)
    half = np.concatenate([later, np.ones((LANE, LANE), np.float32)], axis=1)
    return jnp.asarray(np.concatenate([half, half], axis=0), BF16)


def _rope_tables(pos):
    half = ROPE // 2
    inv = ROPE_THETA ** (-jnp.arange(half, dtype=F32) / half)
    ang = pos.astype(F32)[:, None] * inv[None, :]
    cos, sin = jnp.cos(ang), jnp.sin(ang)
    t = pos.shape[0]
    cos2 = jnp.concatenate([cos, cos], axis=1)
    sin2 = jnp.concatenate([-sin, sin], axis=1)
    one64, z64, z32 = jnp.ones((t, 64), F32), jnp.zeros((t, 64), F32), jnp.zeros((t, 32), F32)
    qa = jnp.concatenate([one64, cos2, z32], axis=1)
    qb = jnp.concatenate([z64, sin2, z32], axis=1)
    kt = jnp.concatenate([cos2, sin2, z64], axis=1)
    return jnp.concatenate([qa, qb, kt], axis=1)


def _gain_rows(q_nope_norm, q_rope_norm, k_rope_norm, k_nope_norm):
    z32, z64 = jnp.zeros((32,), F32), jnp.zeros((64,), F32)
    rows = [
        jnp.concatenate([q_nope_norm, q_rope_norm, z32]),
        jnp.concatenate([z64, jnp.roll(q_rope_norm, 16), z32]),
        jnp.concatenate([k_rope_norm, jnp.roll(k_rope_norm, 16), z64]),
        jnp.concatenate([k_nope_norm, z64]),
    ]
    rows += [jnp.zeros((LANE,), F32)] * 4
    return jnp.stack(rows)


def _proj_body(x_ref, tab_ref, an_ref, win_ref, qan_ref, wuq_ref, seg_ref, gains_ref, kvn_ref, wukv_ref,
               sk_o, sv_o, ckv_o, kr_o, sqb_o, skb_o, svb_o, qm_o, km_o, vm_o, gsb_o, gml_o):
    hb = _rms(x_ref[...], an_ref[...]).astype(BF16)

    def cols(a, b):
        return _dot(hb, win_ref[:, a:b])

    sqb_o[...] = cols(C_SQ, C_SK).astype(BF16)
    sk = cols(C_SK, C_SV)
    sk_o[...] = sk
    skb_o[...] = sk.astype(BF16)
    sv = cols(C_SV, C_DQ)
    sv_o[...] = sv
    svb_o[...] = sv.astype(BF16)
    gsb_o[...] = jax.nn.sigmoid(cols(C_GSB, C_GML))
    gml_o[...] = jax.nn.sigmoid(cols(C_GML, C_KR))

    tab = tab_ref[...]
    gains = gains_ref[...]
    qa = tab[:, 0:LANE] * gains[0:1, :]
    qb = tab[:, LANE:2 * LANE] * gains[1:2, :]
    kt = tab[:, 2 * LANE:3 * LANE] * gains[2:3, :]
    gkn = gains[3:4, :]
    seg = seg_ref[...]

    dqn = _rms(cols(C_DQ, C_DKV), qan_ref[...]).astype(BF16)
    for s in range(MLA_WIDTH // 256):
        q = _dot(dqn, wuq_ref[:, s * 256:(s + 1) * 256])
        qs = _dot(dqn, wuq_ref[:, MLA_WIDTH + s * 256:MLA_WIDTH + (s + 1) * 256])
        r = lax.rsqrt(_dot((q * q).astype(BF16), seg) + EPS)
        qa2 = jnp.concatenate([qa, qa], axis=1)
        qb2 = jnp.concatenate([qb, qb], axis=1)
        qm_o[:, s * 256:(s + 1) * 256] = (r * (q * qa2 + qs * qb2)).astype(BF16)

    krg = cols(C_KR, IN_WIDTH_P)
    lane = lax.broadcasted_iota(jnp.int32, krg.shape, 1)
    ms = jnp.sum(jnp.where(lane < ROPE, krg * krg, 0.0), axis=-1, keepdims=True) * (1.0 / ROPE)
    t = krg * lax.rsqrt(ms + EPS) * kt
    kr_lo = jnp.where(lane < ROPE, t + pltpu.roll(t, LANE - ROPE, 1), 0.0)
    kr_o[...] = kr_lo[:, 0:ROPE]
    kr_slab = pltpu.roll(kr_lo, NOPE, 1)

    ckv = _rms(cols(C_DKV, C_GSB), kvn_ref[...])
    ckv_o[...] = ckv
    ckvb = ckv.astype(BF16)
    gkn2 = jnp.concatenate([gkn, gkn], axis=1)
    kr2 = jnp.concatenate([kr_slab, kr_slab], axis=1)
    for s in range(MLA_WIDTH // 256):
        k = _dot(ckvb, wukv_ref[:, s * 256:(s + 1) * 256])
        r = lax.rsqrt(_dot((k * k).astype(BF16), seg) + EPS)
        km_o[:, s * 256:(s + 1) * 256] = (k * r * gkn2 + kr2).astype(BF16)
        v = _dot(ckvb, wukv_ref[:, MLA_WIDTH + s * 256:MLA_WIDTH + (s + 1) * 256])
        vm_o[:, s * 256:(s + 1) * 256] = v.astype(BF16)


def _proj(x, tab, tab_blocks, tm, an, win, qan, wuq, seg, gains, kvn, wukv, name):
    n = x.shape[0]
    grid = (n // tm,)
    tok = lambda w, dt: jax.ShapeDtypeStruct((n, w), dt)
    out_shape = (tok(512, F32), tok(512, F32), tok(KV_RANK, F32), tok(ROPE, F32),
                 tok(512, BF16), tok(512, BF16), tok(512, BF16),
                 tok(MLA_WIDTH, BF16), tok(MLA_WIDTH, BF16), tok(MLA_WIDTH, BF16),
                 tok(D_MODEL, F32), tok(D_MODEL, F32))
    row = lambda w: pl.BlockSpec((tm, w), lambda i: (i, 0))
    const = lambda a: pl.BlockSpec(a.shape, lambda i: (0,) * a.ndim)
    in_specs = [row(D_MODEL), pl.BlockSpec((tm, 3 * LANE), lambda i: (i % tab_blocks, 0)),
                const(an), const(win), const(qan), const(wuq), const(seg), const(gains), const(kvn), const(wukv)]
    out_specs = tuple(row(s.shape[1]) for s in out_shape)
    return pl.pallas_call(
        _proj_body, grid=grid, in_specs=in_specs, out_specs=out_specs, out_shape=out_shape, name=name,
        compiler_params=pltpu.CompilerParams(dimension_semantics=("arbitrary",), vmem_limit_bytes=VMEM_LIMIT),
    )(x, tab, an, win, qan, wuq, seg, gains, kvn, wukv)


def _sb_logs(z):
    soft = jnp.log1p(jnp.exp(-jnp.abs(z)))
    return jnp.minimum(-z, 0.0) - soft, jnp.minimum(z, 0.0) - soft


def _sb_block(lr, lb, c, w2, valid):
    tk = lr.shape[1]
    ws = [None] * (tk // LANE)
    for s in reversed(range(tk // LANE)):
        x = lr[:, s * LANE:(s + 1) * LANE]
        hi = x.astype(BF16)
        lo = (x - hi.astype(F32)).astype(BF16)
        r = _dot(jnp.concatenate([hi, lo], axis=1), w2)
        w = jnp.exp(lb[:, s * LANE:(s + 1) * LANE] + r[:, :LANE] + c)
        if valid is not None:
            w = jnp.where(valid[:, s * LANE:(s + 1) * LANE], w, 0.0)
        ws[s] = w.astype(BF16)
        c = c + r[:, LANE:]
    return jnp.concatenate(ws, axis=1), c


def _sb_attn_body(q_ref, k_ref, v_ref, km_ref, vm_ref, w2_ref, o_ref, qs_ref, acc_ref, c_ref, *, tq):
    i = pl.program_id(2)
    tk = tq
    lane_q = lax.broadcasted_iota(jnp.int32, (tq, LANE), 1)
    q = q_ref[...]
    zq = jnp.zeros_like(q)
    qs_ref[0:tq, :] = jnp.where(lane_q < HEAD_DIM, q, zq)
    qs_ref[tq:2 * tq, :] = jnp.where(lane_q >= HEAD_DIM, q, zq)
    acc_ref[...] = jnp.zeros_like(acc_ref)
    c_ref[...] = jnp.zeros_like(c_ref)
    w2 = w2_ref[...]

    def process(kb, vb, valid):
        z = _dot_nt(qs_ref[...], kb)
        lr, lb = _sb_logs(z)
        if valid is not None:
            lr = jnp.where(valid, lr, 0.0)
        w, c = _sb_block(lr, lb, c_ref[...], w2, valid)
        c_ref[...] = c
        nk = kb.shape[0]
        wcat = jnp.concatenate([w[:tq], w[tq:]], axis=1)
        lane_k = lax.broadcasted_iota(jnp.int32, (nk, LANE), 1)
        zv = jnp.zeros_like(vb)
        vstack = jnp.concatenate([jnp.where(lane_k < HEAD_DIM, vb, zv),
                                  jnp.where(lane_k >= HEAD_DIM, vb, zv)], axis=0)
        acc_ref[...] += _dot(wcat, vstack)

    row = lax.broadcasted_iota(jnp.int32, (2 * tq, tk), 0)
    col = lax.broadcasted_iota(jnp.int32, (2 * tq, tk), 1)
    row = jnp.where(row >= tq, row - tq, row)
    start = pl.multiple_of(i * tk, tk)
    process(k_ref[pl.ds(start, tk), :], v_ref[pl.ds(start, tk), :], col < row)

    def body(t, carry):
        s = pl.multiple_of((i - 1 - t) * tk, tk)
        process(k_ref[pl.ds(s, tk), :], v_ref[pl.ds(s, tk), :], None)
        return carry

    lax.fori_loop(0, i, body, 0)
    col_m = lax.broadcasted_iota(jnp.int32, (2 * tq, LANE), 1)
    process(km_ref[...], vm_ref[...], col_m < N_META)
    o_ref[...] = acc_ref[...].astype(BF16)


def _sb_attn(q, k, v, k_meta, v_meta, w2, tq):
    b, t, _ = q.shape
    grid = (b, SB_WIDTH // LANE, t // tq)
    qspec = pl.BlockSpec((None, tq, LANE), lambda b_, p, i: (b_, i, p))
    kspec = pl.BlockSpec((None, t, LANE), lambda b_, p, i: (b_, 0, p))
    mspec = pl.BlockSpec((LANE, LANE), lambda b_, p, i: (0, p))
    wspec = pl.BlockSpec(w2.shape, lambda b_, p, i: (0, 0))
    return pl.pallas_call(
        functools.partial(_sb_attn_body, tq=tq), grid=grid,
        in_specs=[qspec, kspec, kspec, mspec, mspec, wspec], out_specs=qspec,
        out_shape=jax.ShapeDtypeStruct(q.shape, BF16), name="sb_attn",
        scratch_shapes=[pltpu.VMEM((2 * tq, LANE), BF16), pltpu.VMEM((tq, LANE), F32),
                        pltpu.VMEM((2 * tq, LANE), F32)],
        compiler_params=pltpu.CompilerParams(dimension_semantics=("arbitrary",) * 3, vmem_limit_bytes=VMEM_LIMIT),
    )(q, k, v, k_meta, v_meta, w2)


def _mla_attn_body(q_ref, k_ref, v_ref, km_ref, vm_ref, o_ref, *, tq):
    i = pl.program_id(2)
    tk = tq
    row = lax.broadcasted_iota(jnp.int32, (tq, tk), 0)
    col = lax.broadcasted_iota(jnp.int32, (tq, tk), 1)
    col_m = lax.broadcasted_iota(jnp.int32, (tq, LANE), 1)
    out = None
    for hh in range(2):
        ls = slice(hh * LANE, (hh + 1) * LANE)
        q = q_ref[:, ls]

        def step(kb, vb, valid, carry):
            m, l, acc = carry
            s = _dot_nt(q, kb) * MLA_SCALE
            if valid is not None:
                s = jnp.where(valid, s, NEG_INF)
            m_new = jnp.maximum(m, jnp.max(s, axis=-1, keepdims=True))
            alpha = jnp.exp(m - m_new)
            p = jnp.exp(s - m_new)
            l = alpha * l + jnp.sum(p, axis=-1, keepdims=True)
            acc = alpha * acc + _dot(p.astype(BF16), vb)
            return m_new, l, acc

        carry = (jnp.full((tq, 1), NEG_INF, F32), jnp.zeros((tq, 1), F32), jnp.zeros((tq, LANE), F32))
        carry = step(km_ref[:, ls], vm_ref[:, ls], col_m < N_META, carry)

        def body(j, carry):
            s = pl.multiple_of(j * tk, tk)
            return step(k_ref[pl.ds(s, tk), ls], v_ref[pl.ds(s, tk), ls], None, carry)

        carry = lax.fori_loop(0, i, body, carry)
        start = pl.multiple_of(i * tk, tk)
        m, l, acc = step(k_ref[pl.ds(start, tk), ls], v_ref[pl.ds(start, tk), ls], col <= row, carry)
        o = acc / l
        out = o if out is None else out + o
    o_ref[...] = out.astype(BF16)


def _mla_attn(q, k, v, k_meta, v_meta, tq):
    b, t, _ = q.shape
    grid = (b, HEADS // 2, t // tq)
    qspec = pl.BlockSpec((None, tq, 2 * LANE), lambda b_, p, i: (b_, i, p))
    kspec = pl.BlockSpec((None, t, 2 * LANE), lambda b_, p, i: (b_, 0, p))
    mspec = pl.BlockSpec((LANE, 2 * LANE), lambda b_, p, i: (0, p))
    ospec = pl.BlockSpec((None, tq, LANE), lambda b_, p, i: (b_, i, p))
    return pl.pallas_call(
        functools.partial(_mla_attn_body, tq=tq), grid=grid,
        in_specs=[qspec, kspec, kspec, mspec, mspec], out_specs=ospec,
        out_shape=jax.ShapeDtypeStruct((b, t, SB_WIDTH), BF16), name="mla_attn",
        compiler_params=pltpu.CompilerParams(dimension_semantics=("arbitrary",) * 3, vmem_limit_bytes=VMEM_LIMIT),
    )(q, k, v, k_meta, v_meta)


ROWS = 16


def _sb_dec_body(pt_ref, q_ref, *refs, pages):
    k_refs, v_refs = refs[:pages], refs[pages:2 * pages]
    w2_ref, o_ref, qbd_ref, acc_ref, c_ref = refs[2 * pages:]
    j = pl.program_id(1)
    row = lax.broadcasted_iota(jnp.int32, (ROWS, SB_WIDTH), 0)
    lane = lax.broadcasted_iota(jnp.int32, (ROWS, SB_WIDTH), 1)
    diag = (lane // HEAD_DIM) == row

    @pl.when(j == 0)
    def _():
        qf = jnp.broadcast_to(q_ref[...].astype(F32), (ROWS, SB_WIDTH))
        qbd_ref[...] = jnp.where(diag, qf, 0.0).astype(BF16)
        acc_ref[...] = jnp.zeros_like(acc_ref)
        c_ref[...] = jnp.zeros_like(c_ref)

    w2 = w2_ref[...]
    qbd = qbd_ref[...]
    c = c_ref[...]
    acc = acc_ref[...]
    for t in range(pages):
        z = _dot(qbd, k_refs[t][...].astype(BF16))
        lr, lb = _sb_logs(z)
        w, c = _sb_block(lr, lb, c, w2, None)
        acc = acc + _dot_nt(w, v_refs[t][...].astype(BF16))
    c_ref[...] = c
    acc_ref[...] = acc

    @pl.when(j == pl.num_programs(1) - 1)
    def _():
        o_ref[...] = jnp.sum(jnp.where(diag, acc, 0.0), axis=0, keepdims=True).astype(BF16)


def _sb_dec(page_table, q, cache_k, cache_v, w2, pages):
    nseq, npages = page_table.shape
    steps = npages // pages

    def page_spec(t):
        return pl.BlockSpec((None, None, SB_WIDTH, PAGE),
                            lambda b, j, pt: (0, pt[b, npages - 1 - (j * pages + t)], 0, 0))

    qspec = pl.BlockSpec((None, 1, SB_WIDTH), lambda b, j, pt: (b, 0, 0))
    in_specs = ([qspec] + [page_spec(t) for t in range(pages)] * 2
                + [pl.BlockSpec(w2.shape, lambda b, j, pt: (0, 0))])
    grid_spec = pltpu.PrefetchScalarGridSpec(
        num_scalar_prefetch=1, grid=(nseq, steps), in_specs=in_specs, out_specs=qspec,
        scratch_shapes=[pltpu.VMEM((ROWS, SB_WIDTH), BF16), pltpu.VMEM((ROWS, SB_WIDTH), F32),
                        pltpu.VMEM((ROWS, LANE), F32)])
    return pl.pallas_call(
        functools.partial(_sb_dec_body, pages=pages), grid_spec=grid_spec,
        out_shape=jax.ShapeDtypeStruct((nseq, 1, SB_WIDTH), BF16), name="sb_dec",
        compiler_params=pltpu.CompilerParams(dimension_semantics=("arbitrary",) * 2, vmem_limit_bytes=VMEM_LIMIT),
    )(page_table, q, *([cache_k] * pages), *([cache_v] * pages), w2)


def _mla_dec_body(pt_ref, qm_ref, kmn_ref, vmn_ref, *refs, pages):
    ckv_refs, kr_refs = refs[:pages], refs[pages:2 * pages]
    (wukt_ref, wuktp_ref, wuv_ref, gains_ref, o_ref,
     wext_ref, q16_ref, qr_ref, m_ref, l_ref, u_ref) = refs[2 * pages:]
    j = pl.program_id(1)
    z8 = jnp.zeros((HEADS, LANE), F32)

    @pl.when(j == 0)
    def _():
        q16 = jnp.concatenate([qm_ref[...].astype(F32), z8], axis=0)
        q16_ref[...] = q16
        qr_ref[...] = q16[:, NOPE:NOPE + ROPE].astype(BF16)
        qg = q16 * gains_ref[3:4, :]
        row = lax.broadcasted_iota(jnp.int32, (ROWS, MLA_WIDTH), 0)
        lane = lax.broadcasted_iota(jnp.int32, (ROWS, MLA_WIDTH), 1)
        qbd = jnp.where((lane // HEAD_SLAB) == row, jnp.concatenate([qg] * HEADS, axis=1), 0.0)
        qabs = _dot(qbd.astype(BF16), wuktp_ref[...])
        wext_ref[0:SB_WIDTH, :] = wukt_ref[...]
        wext_ref[SB_WIDTH:SB_WIDTH + ROWS, :] = qabs.astype(BF16)
        m_ref[...] = jnp.full_like(m_ref, NEG_INF)
        l_ref[...] = jnp.zeros_like(l_ref)
        u_ref[...] = jnp.zeros_like(u_ref)

    wext = wext_ref[...]
    qr = qr_ref[...]
    m, l, u = m_ref[...], l_ref[...], u_ref[...]
    ones8 = jnp.ones((HEADS, PAGE), F32)
    for t in range(pages):
        ckvb = ckv_refs[t][...].astype(BF16)
        kt = _dot_nt(wext, ckvb)
        kv3 = kt[0:SB_WIDTH].reshape(HEADS, NOPE, PAGE)
        ssq = jnp.sum(kv3 * kv3, axis=1)
        r = jnp.concatenate([lax.rsqrt(ssq * (1.0 / NOPE) + EPS), ones8], axis=0)
        s_rope = _dot(qr, kr_refs[t][...].astype(BF16))
        s = (kt[SB_WIDTH:SB_WIDTH + ROWS] * r + s_rope) * MLA_SCALE
        m_new = jnp.maximum(m, jnp.max(s, axis=-1, keepdims=True))
        alpha = jnp.exp(m - m_new)
        p = jnp.exp(s - m_new)
        l = alpha * l + jnp.sum(p, axis=-1, keepdims=True)
        u = alpha * u + _dot(p.astype(BF16), ckvb)
        m = m_new
    m_ref[...], l_ref[...], u_ref[...] = m, l, u

    @pl.when(j == pl.num_programs(1) - 1)
    def _():
        k16 = jnp.concatenate([kmn_ref[...].astype(F32), z8], axis=0)
        v16 = jnp.concatenate([vmn_ref[...].astype(F32), z8], axis=0)
        s_new = jnp.sum(q16_ref[...] * k16, axis=-1, keepdims=True) * MLA_SCALE
        m_f = jnp.maximum(m, s_new)
        alpha = jnp.exp(m - m_f)
        p_new = jnp.exp(s_new - m_f)
        l_f = alpha * l + p_new
        o_full = _dot((alpha * u / l_f).astype(BF16), wuv_ref[...])
        row = lax.broadcasted_iota(jnp.int32, (ROWS, SB_WIDTH), 0)
        lane = lax.broadcasted_iota(jnp.int32, (ROWS, SB_WIDTH), 1)
        new = jnp.concatenate([v16 * (p_new / l_f)] * (SB_WIDTH // LANE), axis=1)
        tot = jnp.where((lane // V_DIM) == row, o_full, 0.0) + jnp.where((lane // LANE) == (row // 2), new, 0.0)
        o_ref[...] = jnp.sum(tot, axis=0, keepdims=True).astype(BF16)


def _mla_dec(page_table, qm, kmn, vmn, cache_ckv, cache_kr, wukt, wuktp, wuv, gains, pages):
    nseq, npages = page_table.shape
    steps = npages // pages

    def page_spec(t, shape):
        return pl.BlockSpec((None, None) + shape, lambda b, j, pt: (0, pt[b, j * pages + t], 0, 0))

    hspec = pl.BlockSpec((None, HEADS, LANE), lambda b, j, pt: (b, 0, 0))
    const = lambda a: pl.BlockSpec(a.shape, lambda b, j, pt: (0,) * a.ndim)
    ospec = pl.BlockSpec((None, 1, SB_WIDTH), lambda b, j, pt: (b, 0, 0))
    in_specs = ([hspec] * 3 + [page_spec(t, (PAGE, KV_RANK)) for t in range(pages)]
                + [page_spec(t, (ROPE, PAGE)) for t in range(pages)]
                + [const(wukt), const(wuktp), const(wuv), const(gains)])
    grid_spec = pltpu.PrefetchScalarGridSpec(
        num_scalar_prefetch=1, grid=(nseq, steps), in_specs=in_specs, out_specs=ospec,
        scratch_shapes=[pltpu.VMEM((SB_WIDTH + ROWS, KV_RANK), BF16), pltpu.VMEM((ROWS, LANE), F32),
                        pltpu.VMEM((ROWS, ROPE), BF16), pltpu.VMEM((ROWS, 1), F32),
                        pltpu.VMEM((ROWS, 1), F32), pltpu.VMEM((ROWS, KV_RANK), F32)])
    return pl.pallas_call(
        functools.partial(_mla_dec_body, pages=pages), grid_spec=grid_spec,
        out_shape=jax.ShapeDtypeStruct((nseq, 1, SB_WIDTH), BF16), name="mla_dec",
        compiler_params=pltpu.CompilerParams(dimension_semantics=("arbitrary",) * 2, vmem_limit_bytes=VMEM_LIMIT),
    )(page_table, qm, kmn, vmn, *([cache_ckv] * pages), *([cache_kr] * pages), wukt, wuktp, wuv, gains)


def _ffn_body(x_ref, osb_ref, oml_ref, gsb_ref, gml_ref, wsbo_ref, wmlo_ref, wout_ref, fn_ref, wgu_ref, wdn_ref,
              y_ref):
    y = gsb_ref[...] * _dot(osb_ref[...], wsbo_ref[...]) + gml_ref[...] * _dot(oml_ref[...], wmlo_ref[...])
    x2 = x_ref[...] + _dot(y.astype(BF16), wout_ref[...])
    hb = _rms(x2, fn_ref[...]).astype(BF16)
    gate = _dot(hb, wgu_ref[:, 0:D_FF])
    up = _dot(hb, wgu_ref[:, D_FF:2 * D_FF])
    act = (jax.nn.silu(gate) * up).astype(BF16)
    y_ref[...] = x2 + _dot(act, wdn_ref[...])


def _ffn(x, osb, oml, gsb, gml, wsbo, wmlo, wout, fn, wgu, wdn, tm, name):
    n = x.shape[0]
    row = lambda w: pl.BlockSpec((tm, w), lambda i: (i, 0))
    const = lambda a: pl.BlockSpec(a.shape, lambda i: (0,) * a.ndim, pipeline_mode=pl.Buffered(1))
    in_specs = [row(D_MODEL), row(SB_WIDTH), row(SB_WIDTH), row(D_MODEL), row(D_MODEL),
                const(wsbo), const(wmlo), const(wout), const(fn), const(wgu), const(wdn)]
    return pl.pallas_call(
        _ffn_body, grid=(n // tm,), in_specs=in_specs, out_specs=row(D_MODEL),
        out_shape=jax.ShapeDtypeStruct((n, D_MODEL), F32), name=name,
        compiler_params=pltpu.CompilerParams(dimension_semantics=("arbitrary",), vmem_limit_bytes=VMEM_LIMIT),
    )(x, osb, oml, gsb, gml, wsbo, wmlo, wout, fn, wgu, wdn)


PROJ_TILE = 256
ATTN_TILE = 256
FFN_TILE = 256
DEC_PAGES = 8


def _pad_rows(a, rows):
    return jnp.concatenate([a, jnp.zeros((rows - a.shape[0],) + a.shape[1:], a.dtype)], axis=0)


def kernel(x_prompt, x_sample, cache_sb_k, cache_sb_v, cache_mla_ckv, cache_mla_krope, page_table, meta_tokens, attn_norm, w_in, q_a_norm, w_uq, q_nope_norm, q_rope_norm, kv_a_norm, k_rope_norm, w_ukv, k_nope_norm, w_sb_o, w_mla_o, w_out, ffn_norm, w_gu, w_down):
    depth = w_in.shape[0]
    assert depth == 1, "one layer per step"
    b, seq, d = x_prompt.shape
    nseq = x_sample.shape[0]
    npool = cache_sb_k.shape[1]
    past = page_table.shape[1] * PAGE

    win = _prep_w_in(w_in[0])
    wuq = _prep_w_uq(w_uq[0])
    wukv, wukt, wuktp, wuv = _prep_w_ukv(w_ukv[0])
    seg, w2 = _seg_matrix(), _cumsum_matrix()
    gains = _gain_rows(q_nope_norm[0], q_rope_norm[0], k_rope_norm[0], k_nope_norm[0])
    row = lambda a: a.reshape(1, -1)
    an, qan, kvn, fn = row(attn_norm[0]), row(q_a_norm[0]), row(kv_a_norm[0]), row(ffn_norm[0])
    wsbo, wmlo, wout = w_sb_o[0].astype(BF16), w_mla_o[0].astype(BF16), w_out[0].astype(BF16)
    wgu, wdn = w_gu[0].astype(BF16), w_down[0].astype(BF16)
    proj_w = (an, win, qan, wuq, seg, gains, kvn, wukv)

    tab_meta = _rope_tables(jnp.arange(N_META))
    tab_real = _rope_tables(N_META + jnp.arange(seq))
    tab_samp = _rope_tables(jnp.full((nseq,), past))

    xr = x_prompt.reshape(b * seq, d)
    (sk, sv, ckv, kr, sqb, skb, svb, qm, km, vm, gsb, gml) = _proj(
        xr, tab_real, seq // PROJ_TILE, PROJ_TILE, *proj_w, name="proj_prompt")
    (sk_m, sv_m, ckv_m, kr_m, _, skb_m, svb_m, _, km_m, vm_m, _, _) = _proj(
        meta_tokens.astype(F32), tab_meta, 1, N_META, *proj_w, name="proj_meta")
    r3 = lambda a: a.reshape(b, seq, a.shape[-1])
    o_sb = _sb_attn(r3(sqb), r3(skb), r3(svb), _pad_rows(skb_m, LANE), _pad_rows(svb_m, LANE), w2, ATTN_TILE)
    o_ml = _mla_attn(r3(qm), r3(km), r3(vm), _pad_rows(km_m, LANE), _pad_rows(vm_m, LANE), ATTN_TILE)
    ffn_w = (wsbo, wmlo, wout, fn, wgu, wdn)
    y_prompt = _ffn(xr, o_sb.reshape(b * seq, -1), o_ml.reshape(b * seq, -1), gsb, gml, *ffn_w,
                    tm=FFN_TILE, name="ffn_prompt").reshape(b, seq, d)

    xs = x_sample.reshape(nseq, d)
    (sk_s, sv_s, ckv_s, kr_s, sqb_s, _, _, qm_s, km_s, vm_s, gsb_s, gml_s) = _proj(
        xs, tab_samp, 1, nseq, *proj_w, name="proj_sample")
    page_t = lambda c: jnp.transpose(c, (0, 1, 3, 4, 2)).reshape(depth, npool, SB_WIDTH, PAGE)
    o_sb_s = _sb_dec(page_table, sqb_s.reshape(nseq, 1, SB_WIDTH), page_t(cache_sb_k), page_t(cache_sb_v),
                     w2, DEC_PAGES)
    h3 = lambda a: a.reshape(nseq, HEADS, LANE)
    o_ml_s = _mla_dec(page_table, h3(qm_s), h3(km_s), h3(vm_s), cache_mla_ckv,
                      jnp.transpose(cache_mla_krope, (0, 1, 3, 2)), wukt, wuktp, wuv, gains, DEC_PAGES)
    y_sample = _ffn(xs, o_sb_s.reshape(nseq, -1), o_ml_s.reshape(nseq, -1), gsb_s, gml_s, *ffn_w,
                    tm=nseq, name="ffn_sample").reshape(nseq, 1, d)

    def prompt_rows(meta, real, shape):
        m = jnp.broadcast_to(meta[None], (b,) + meta.shape)
        return jnp.concatenate([m, real.reshape(b, seq, -1)], axis=1).reshape((1, b, N_META + seq) + shape)

    hd = (HEADS, HEAD_DIM)
    return (y_prompt, y_sample,
            prompt_rows(sk_m, sk, hd), prompt_rows(sv_m, sv, hd),
            prompt_rows(ckv_m, ckv, (KV_RANK,)), prompt_rows(kr_m, kr, (ROPE,)),
            sk_s.reshape((1, nseq, 1) + hd), sv_s.reshape((1, nseq, 1) + hd),
            ckv_s.reshape(1, nseq, 1, KV_RANK), kr_s.reshape(1, nseq, 1, ROPE))
```

```python
import functools

import numpy as np
import jax
import jax.numpy as jnp
from jax import lax
from jax.experimental import pallas as pl
from jax.experimental.pallas import tpu as pltpu

F32 = jnp.float32
BF16 = jnp.bfloat16

D_MODEL = 1024
N_META = 16
EPS = 1e-6
NEG_INF = -1e30
HEADS = 8
HEAD_DIM = 64
SB_WIDTH = HEADS * HEAD_DIM
SB_SCALE = HEAD_DIM ** -0.5
Q_RANK = 384
KV_RANK = 256
NOPE = 64
ROPE = 32
V_DIM = 64
MLA_SCALE = (NOPE + ROPE) ** -0.5
MLA_EXP2_SCALE = MLA_SCALE * float(np.log2(np.e))
ROPE_THETA = 10000.0
D_FF = 2816
PAGE = 128
IN_SPLITS = (SB_WIDTH, SB_WIDTH, SB_WIDTH, Q_RANK, KV_RANK, ROPE, D_MODEL, D_MODEL)
SPLIT_POINTS = tuple(int(v) for v in np.cumsum(IN_SPLITS)[:-1])

LANE = 128
HEAD_SLAB = LANE
MLA_WIDTH = HEADS * HEAD_SLAB
C_SQ, C_SK, C_SV = 0, 512, 1024
C_DQ, C_DKV = 1536, 1920
C_GSB, C_GML, C_KR = 2176, 3200, 4224
IN_WIDTH_P = 4352
VMEM_LIMIT = 56 * 1024 * 1024

NT_DIMS = (((1,), (1,)), ((), ()))


def _dot(a, b):
    return jnp.dot(a, b, preferred_element_type=F32)


def _dot_nt(a, b):
    return lax.dot_general(a, b, NT_DIMS, preferred_element_type=F32)


def _rms(x, g):
    return x * lax.rsqrt(jnp.mean(x * x, axis=-1, keepdims=True) + EPS) * g


def _prep_w_in(w):
    sq, sk, sv, dq, dkv, kr, gsb, gml = jnp.split(w, SPLIT_POINTS, axis=1)
    krg = jnp.concatenate([kr, jnp.roll(kr, 16, axis=1), jnp.zeros((D_MODEL, 64), F32)], axis=1)
    return jnp.concatenate([sq * SB_SCALE, sk, sv, dq, dkv, gsb, gml, krg], axis=1).astype(BF16)


def _prep_w_uq(w):
    w = w.reshape(Q_RANK, HEADS, NOPE + ROPE)
    nope, rope = w[..., :NOPE], w[..., NOPE:]
    z32 = jnp.zeros((Q_RANK, HEADS, 32), F32)
    z64 = jnp.zeros((Q_RANK, HEADS, 64), F32)
    main = jnp.concatenate([nope, rope, z32], axis=-1).reshape(Q_RANK, MLA_WIDTH)
    swap = jnp.concatenate([z64, jnp.roll(rope, 16, axis=-1), z32], axis=-1).reshape(Q_RANK, MLA_WIDTH)
    return jnp.concatenate([main, swap], axis=1).astype(BF16)


def _prep_w_ukv(w):
    w = w.reshape(KV_RANK, HEADS, NOPE + V_DIM)
    wk, wv = w[..., :NOPE], w[..., NOPE:]
    z64 = jnp.zeros((KV_RANK, HEADS, 64), F32)
    k_lay = jnp.concatenate([wk, z64], axis=-1).reshape(KV_RANK, MLA_WIDTH)
    lo = jnp.concatenate([wv, z64], axis=-1)
    hi = jnp.concatenate([z64, wv], axis=-1)
    odd = (jnp.arange(HEADS) % 2 == 1)[None, :, None]
    v_lay = jnp.where(odd, hi, lo).reshape(KV_RANK, MLA_WIDTH)
    w_proj = jnp.concatenate([k_lay, v_lay], axis=1).astype(BF16)
    wuk_t = wk.reshape(KV_RANK, HEADS * NOPE).T.astype(BF16)
    wuk_t_pad = k_lay.T.astype(BF16)
    wuv = wv.reshape(KV_RANK, HEADS * V_DIM).astype(BF16)
    return w_proj, wuk_t, wuk_t_pad, wuv


def _seg_matrix():
    m = np.zeros((LANE, LANE), np.float32)
    m[:NOPE, :NOPE] = 1.0 / NOPE
    m[NOPE:NOPE + ROPE, NOPE:NOPE + ROPE] = 1.0 / ROPE
    z = np.zeros_like(m)
    return jnp.asarray(np.block([[m, z], [z, m]]), BF16)


def _cumsum_matrix():
    j = np.arange(LANE)
    later = (j[:, None] > j[None, :]).astype(np.float32)
    half = np.concatenate([later, np.ones((LANE, LANE), np.float32)], axis=1)
    return jnp.asarray(np.concatenate([half, half], axis=0), BF16)


def _rope_tables(pos):
    half = ROPE // 2
    inv = ROPE_THETA ** (-jnp.arange(half, dtype=F32) / half)
    ang = pos.astype(F32)[:, None] * inv[None, :]
    cos, sin = jnp.cos(ang), jnp.sin(ang)
    t = pos.shape[0]
    cos2 = jnp.concatenate([cos, cos], axis=1)
    sin2 = jnp.concatenate([-sin, sin], axis=1)
    one64, z64, z32 = jnp.ones((t, 64), F32), jnp.zeros((t, 64), F32), jnp.zeros((t, 32), F32)
    qa = jnp.concatenate([one64, cos2, z32], axis=1)
    qb = jnp.concatenate([z64, sin2, z32], axis=1)
    kt = jnp.concatenate([cos2, sin2, z64], axis=1)
    return jnp.concatenate([qa, qb, kt], axis=1)


def _gain_rows(q_nope_norm, q_rope_norm, k_rope_norm, k_nope_norm):
    z32, z64 = jnp.zeros((32,), F32), jnp.zeros((64,), F32)
    rows = [
        jnp.concatenate([q_nope_norm, q_rope_norm, z32]),
        jnp.concatenate([z64, jnp.roll(q_rope_norm, 16), z32]),
        jnp.concatenate([k_rope_norm, jnp.roll(k_rope_norm, 16), z64]),
        jnp.concatenate([k_nope_norm, z64]),
    ]
    rows += [jnp.zeros((LANE,), F32)] * 4
    return jnp.stack(rows)


def _proj_body(x_ref, tab_ref, an_ref, win_ref, qan_ref, wuq_ref, seg_ref, gains_ref, kvn_ref, wukv_ref,
               sk_o, sv_o, ckv_o, kr_o, sqb_o, skb_o, svb_o, qm_o, km_o, vm_o, gsb_o, gml_o):
    hb = _rms(x_ref[...], an_ref[...]).astype(BF16)

    def cols(a, b):
        return _dot(hb, win_ref[:, a:b])

    sqb_o[...] = cols(C_SQ, C_SK).astype(BF16)
    sk = cols(C_SK, C_SV)
    sk_o[...] = sk
    skb_o[...] = sk.astype(BF16)
    sv = cols(C_SV, C_DQ)
    sv_o[...] = sv
    svb_o[...] = sv.astype(BF16)
    gsb_o[...] = jax.nn.sigmoid(cols(C_GSB, C_GML))
    gml_o[...] = jax.nn.sigmoid(cols(C_GML, C_KR))

    tab = tab_ref[...]
    gains = gains_ref[...]
    qa = tab[:, 0:LANE] * gains[0:1, :]
    qb = tab[:, LANE:2 * LANE] * gains[1:2, :]
    kt = tab[:, 2 * LANE:3 * LANE] * gains[2:3, :]
    gkn = gains[3:4, :]
    seg = seg_ref[...]

    dqn = _rms(cols(C_DQ, C_DKV), qan_ref[...]).astype(BF16)
    for s in range(MLA_WIDTH // 256):
        q = _dot(dqn, wuq_ref[:, s * 256:(s + 1) * 256])
        qs = _dot(dqn, wuq_ref[:, MLA_WIDTH + s * 256:MLA_WIDTH + (s + 1) * 256])
        r = lax.rsqrt(_dot((q * q).astype(BF16), seg) + EPS)
        qa2 = jnp.concatenate([qa, qa], axis=1)
        qb2 = jnp.concatenate([qb, qb], axis=1)
        qm_o[:, s * 256:(s + 1) * 256] = (r * (q * qa2 + qs * qb2)).astype(BF16)

    krg = cols(C_KR, IN_WIDTH_P)
    lane = lax.broadcasted_iota(jnp.int32, krg.shape, 1)
    ms = jnp.sum(jnp.where(lane < ROPE, krg * krg, 0.0), axis=-1, keepdims=True) * (1.0 / ROPE)
    t = krg * lax.rsqrt(ms + EPS) * kt
    kr_lo = jnp.where(lane < ROPE, t + pltpu.roll(t, LANE - ROPE, 1), 0.0)
    kr_o[...] = kr_lo[:, 0:ROPE]
    kr_slab = pltpu.roll(kr_lo, NOPE, 1)

    ckv = _rms(cols(C_DKV, C_GSB), kvn_ref[...])
    ckv_o[...] = ckv
    ckvb = ckv.astype(BF16)
    gkn2 = jnp.concatenate([gkn, gkn], axis=1)
    kr2 = jnp.concatenate([kr_slab, kr_slab], axis=1)
    lane2 = lax.broadcasted_iota(jnp.int32, (krg.shape[0], 2 * LANE), 1)
    free_half = jnp.logical_and(lane2 >= V_DIM, lane2 < 2 * LANE - V_DIM)
    for s in range(MLA_WIDTH // 256):
        k = _dot(ckvb, wukv_ref[:, s * 256:(s + 1) * 256])
        r = lax.rsqrt(_dot((k * k).astype(BF16), seg) + EPS)
        km_o[:, s * 256:(s + 1) * 256] = (k * r * gkn2 + kr2).astype(BF16)
        v = _dot(ckvb, wukv_ref[:, MLA_WIDTH + s * 256:MLA_WIDTH + (s + 1) * 256])
        vm_o[:, s * 256:(s + 1) * 256] = jnp.where(free_half, 1.0, v).astype(BF16)


def _proj(x, tab, tab_blocks, tm, an, win, qan, wuq, seg, gains, kvn, wukv, name):
    n = x.shape[0]
    grid = (n // tm,)
    tok = lambda w, dt: jax.ShapeDtypeStruct((n, w), dt)
    out_shape = (tok(512, F32), tok(512, F32), tok(KV_RANK, F32), tok(ROPE, F32),
                 tok(512, BF16), tok(512, BF16), tok(512, BF16),
                 tok(MLA_WIDTH, BF16), tok(MLA_WIDTH, BF16), tok(MLA_WIDTH, BF16),
                 tok(D_MODEL, F32), tok(D_MODEL, F32))
    row = lambda w: pl.BlockSpec((tm, w), lambda i: (i, 0))
    const = lambda a: pl.BlockSpec(a.shape, lambda i: (0,) * a.ndim)
    in_specs = [row(D_MODEL), pl.BlockSpec((tm, 3 * LANE), lambda i: (i % tab_blocks, 0)),
                const(an), const(win), const(qan), const(wuq), const(seg), const(gains), const(kvn), const(wukv)]
    out_specs = tuple(row(s.shape[1]) for s in out_shape)
    return pl.pallas_call(
        _proj_body, grid=grid, in_specs=in_specs, out_specs=out_specs, out_shape=out_shape, name=name,
        compiler_params=pltpu.CompilerParams(dimension_semantics=("arbitrary",), vmem_limit_bytes=VMEM_LIMIT),
    )(x, tab, an, win, qan, wuq, seg, gains, kvn, wukv)


SB_EXIT = -110.0


def _sb_logs(z):
    soft = jnp.log(1.0 + jnp.exp(-jnp.abs(z)))
    return jnp.minimum(-z, 0.0) - soft, jnp.minimum(z, 0.0) - soft


def _sb_block(lr, lb, c, w2, valid):
    tk = lr.shape[1]
    ws = [None] * (tk // LANE)
    for s in reversed(range(tk // LANE)):
        x = lr[:, s * LANE:(s + 1) * LANE]
        hi = x.astype(BF16)
        lo = (x - hi.astype(F32)).astype(BF16)
        r = _dot(jnp.concatenate([hi, lo], axis=1), w2)
        w = jnp.exp(lb[:, s * LANE:(s + 1) * LANE] + r[:, :LANE] + c)
        if valid is not None:
            w = jnp.where(valid[:, s * LANE:(s + 1) * LANE], w, 0.0)
        ws[s] = w.astype(BF16)
        c = c + r[:, LANE:]
    return jnp.concatenate(ws, axis=1), c


def _sb_attn_body(q_ref, k_ref, v_ref, km_ref, vm_ref, w2_ref, o_ref, qs_ref, acc_ref, c_ref, *, tq):
    i = pl.program_id(2)
    tk = tq
    lane_q = lax.broadcasted_iota(jnp.int32, (tq, LANE), 1)
    q = q_ref[...]
    zq = jnp.zeros_like(q)
    qs_ref[0:tq, :] = jnp.where(lane_q < HEAD_DIM, q, zq)
    qs_ref[tq:2 * tq, :] = jnp.where(lane_q >= HEAD_DIM, q, zq)
    acc_ref[...] = jnp.zeros_like(acc_ref)
    c_ref[...] = jnp.zeros_like(c_ref)
    w2 = w2_ref[...]

    def process(kb, vb, valid):
        z = _dot_nt(qs_ref[...], kb)
        lr, lb = _sb_logs(z)
        if valid is not None:
            lr = jnp.where(valid, lr, 0.0)
        w, c = _sb_block(lr, lb, c_ref[...], w2, valid)
        c_ref[...] = c
        nk = kb.shape[0]
        wcat = jnp.concatenate([w[:tq], w[tq:]], axis=1)
        lane_k = lax.broadcasted_iota(jnp.int32, (nk, LANE), 1)
        zv = jnp.zeros_like(vb)
        vstack = jnp.concatenate([jnp.where(lane_k < HEAD_DIM, vb, zv),
                                  jnp.where(lane_k >= HEAD_DIM, vb, zv)], axis=0)
        acc_ref[...] += _dot(wcat, vstack)

    row = lax.broadcasted_iota(jnp.int32, (2 * tq, tk), 0)
    col = lax.broadcasted_iota(jnp.int32, (2 * tq, tk), 1)
    row = jnp.where(row >= tq, row - tq, row)
    start = pl.multiple_of(i * tk, tk)
    process(k_ref[pl.ds(start, tk), :], v_ref[pl.ds(start, tk), :], col < row)

    def live():
        return jnp.max(c_ref[...]) > SB_EXIT

    def body(state):
        t, _ = state
        s = pl.multiple_of((i - 1 - t) * tk, tk)
        process(k_ref[pl.ds(s, tk), :], v_ref[pl.ds(s, tk), :], None)
        return t + 1, live()

    _, go = lax.while_loop(lambda st: jnp.logical_and(st[0] < i, st[1]), body, (jnp.int32(0), live()))

    @pl.when(go)
    def _():
        col_m = lax.broadcasted_iota(jnp.int32, (2 * tq, LANE), 1)
        process(km_ref[...], vm_ref[...], col_m < N_META)

    o_ref[...] = acc_ref[...].astype(BF16)


def _sb_attn(q, k, v, k_meta, v_meta, w2, tq):
    b, t, _ = q.shape
    grid = (b, SB_WIDTH // LANE, t // tq)
    qspec = pl.BlockSpec((None, tq, LANE), lambda b_, p, i: (b_, i, p))
    kspec = pl.BlockSpec((None, t, LANE), lambda b_, p, i: (b_, 0, p))
    mspec = pl.BlockSpec((LANE, LANE), lambda b_, p, i: (0, p))
    wspec = pl.BlockSpec(w2.shape, lambda b_, p, i: (0, 0))
    return pl.pallas_call(
        functools.partial(_sb_attn_body, tq=tq), grid=grid,
        in_specs=[qspec, kspec, kspec, mspec, mspec, wspec], out_specs=qspec,
        out_shape=jax.ShapeDtypeStruct(q.shape, BF16), name="sb_attn",
        scratch_shapes=[pltpu.VMEM((2 * tq, LANE), BF16), pltpu.VMEM((tq, LANE), F32),
                        pltpu.VMEM((2 * tq, LANE), F32)],
        compiler_params=pltpu.CompilerParams(dimension_semantics=("arbitrary",) * 3, vmem_limit_bytes=VMEM_LIMIT),
    )(q, k, v, k_meta, v_meta, w2)


def _mla_attn_body(q_ref, k_ref, v_ref, km_ref, vm_ref, o_ref, *, tq, nh):
    i = pl.program_id(2)
    tk = tq
    row = lax.broadcasted_iota(jnp.int32, (tq, tk), 0)
    col = lax.broadcasted_iota(jnp.int32, (tq, tk), 1)
    col_m = lax.broadcasted_iota(jnp.int32, (tq, LANE), 1)
    slabs = [slice(h * LANE, (h + 1) * LANE) for h in range(nh)]
    qs = [q_ref[:, ls] for ls in slabs]

    def scores(h, kb, valid):
        s = _dot_nt(qs[h], kb[:, slabs[h]])
        return s if valid is None else jnp.where(valid, s, NEG_INF)

    def sweep(fn, carry):
        carry = fn(km_ref, vm_ref, 0, LANE, col_m < N_META, carry)
        carry = lax.fori_loop(0, i, lambda j, c: fn(k_ref, v_ref, pl.multiple_of(j * tk, tk), tk, None, c), carry)
        return fn(k_ref, v_ref, pl.multiple_of(i * tk, tk), tk, col <= row, carry)

    def step(kr_, vr_, start, size, valid, carry):
        kb = kr_[pl.ds(start, size), :]
        vb = vr_[pl.ds(start, size), :]
        out = []
        for h in range(nh):
            m, a = carry[h]
            s = scores(h, kb, valid)
            m_new = jnp.maximum(m, jnp.max(s, axis=-1, keepdims=True))
            alpha = jnp.exp2((m - m_new) * MLA_EXP2_SCALE)
            p = jnp.exp2((s - m_new) * MLA_EXP2_SCALE)
            out.append((m_new, alpha * a + _dot(p.astype(BF16), vb[:, slabs[h]])))
        return tuple(out)

    init = (jnp.full((tq, 1), NEG_INF, F32), jnp.zeros((tq, LANE), F32))
    acc = [r[1] for r in sweep(step, (init,) * nh)]
    outs = [a / pltpu.roll(a, V_DIM, 1) for a in acc]
    pairs = [jnp.where(col_m < V_DIM, outs[h], outs[h + 1]) for h in range(0, nh, 2)]
    o_ref[...] = jnp.concatenate(pairs, axis=1).astype(BF16)


def _mla_attn(q, k, v, k_meta, v_meta, tq, nh):
    b, t, _ = q.shape
    grid = (b, HEADS // nh, t // tq)
    qspec = pl.BlockSpec((None, tq, nh * LANE), lambda b_, p, i: (b_, i, p))
    kspec = pl.BlockSpec((None, t, nh * LANE), lambda b_, p, i: (b_, 0, p))
    mspec = pl.BlockSpec((LANE, nh * LANE), lambda b_, p, i: (0, p))
    ospec = pl.BlockSpec((None, tq, nh * V_DIM), lambda b_, p, i: (b_, i, p))
    return pl.pallas_call(
        functools.partial(_mla_attn_body, tq=tq, nh=nh), grid=grid,
        in_specs=[qspec, kspec, kspec, mspec, mspec], out_specs=ospec,
        out_shape=jax.ShapeDtypeStruct((b, t, SB_WIDTH), BF16), name="mla_attn",
        compiler_params=pltpu.CompilerParams(dimension_semantics=("arbitrary",) * 3, vmem_limit_bytes=VMEM_LIMIT),
    )(q, k, v, k_meta, v_meta)


ROWS = 16


def _sb_dec_body(pt_ref, q_ref, *refs, pages):
    k_refs, v_refs = refs[:pages], refs[pages:2 * pages]
    k_hbm, v_hbm, w2_ref, o_ref, kbuf, vbuf, sem = refs[2 * pages:]
    b = pl.program_id(0)
    npages = pt_ref.shape[1]
    row = lax.broadcasted_iota(jnp.int32, (ROWS, SB_WIDTH), 0)
    lane = lax.broadcasted_iota(jnp.int32, (ROWS, SB_WIDTH), 1)
    diag = (lane // HEAD_DIM) == row
    head_row = lax.broadcasted_iota(jnp.int32, (ROWS, LANE), 0) < HEADS
    qbd = jnp.where(diag, jnp.broadcast_to(q_ref[...].astype(F32), (ROWS, SB_WIDTH)), 0.0).astype(BF16)
    w2 = w2_ref[...]

    def page(kt, vt, c, acc):
        z = _dot(qbd, kt.astype(BF16))
        lr, lb = _sb_logs(z)
        w, c = _sb_block(lr, lb, c, w2, None)
        return c, acc + _dot_nt(w, vt.astype(BF16))

    def live(c):
        return jnp.max(jnp.where(head_row, c, SB_EXIT)) > SB_EXIT

    c = jnp.zeros((ROWS, LANE), F32)
    acc = jnp.zeros((ROWS, SB_WIDTH), F32)
    for t in range(pages):
        c, acc = page(k_refs[t][...], v_refs[t][...], c, acc)

    def older(state):
        n, _, c, acc = state
        pg = pt_ref[b, npages - 1 - n]
        ck = pltpu.make_async_copy(k_hbm.at[0, pg], kbuf, sem.at[0])
        cv = pltpu.make_async_copy(v_hbm.at[0, pg], vbuf, sem.at[1])
        ck.start()
        cv.start()
        ck.wait()
        cv.wait()
        c, acc = page(kbuf[...], vbuf[...], c, acc)
        return n + 1, live(c), c, acc

    _, _, c, acc = lax.while_loop(lambda st: jnp.logical_and(st[0] < npages, st[1]), older,
                                  (jnp.int32(pages), live(c), c, acc))
    o_ref[...] = jnp.sum(jnp.where(diag, acc, 0.0), axis=0, keepdims=True).astype(BF16)


def _sb_dec(page_table, q, cache_k, cache_v, w2, pages):
    nseq, npages = page_table.shape

    def page_spec(t):
        return pl.BlockSpec((None, None, SB_WIDTH, PAGE), lambda b, pt: (0, pt[b, npages - 1 - t], 0, 0))

    qspec = pl.BlockSpec((None, 1, SB_WIDTH), lambda b, pt: (b, 0, 0))
    anyspec = pl.BlockSpec(memory_space=pl.ANY)
    in_specs = ([qspec] + [page_spec(t) for t in range(pages)] * 2
                + [anyspec, anyspec, pl.BlockSpec(w2.shape, lambda b, pt: (0, 0))])
    grid_spec = pltpu.PrefetchScalarGridSpec(
        num_scalar_prefetch=1, grid=(nseq,), in_specs=in_specs, out_specs=qspec,
        scratch_shapes=[pltpu.VMEM((SB_WIDTH, PAGE), F32), pltpu.VMEM((SB_WIDTH, PAGE), F32),
                        pltpu.SemaphoreType.DMA((2,))])
    return pl.pallas_call(
        functools.partial(_sb_dec_body, pages=pages), grid_spec=grid_spec,
        out_shape=jax.ShapeDtypeStruct((nseq, 1, SB_WIDTH), BF16), name="sb_dec",
        compiler_params=pltpu.CompilerParams(dimension_semantics=("arbitrary",), vmem_limit_bytes=VMEM_LIMIT),
    )(page_table, q, *([cache_k] * pages), *([cache_v] * pages), cache_k, cache_v, w2)


def _mla_dec_body(pt_ref, qm_ref, kmn_ref, vmn_ref, *refs, pages):
    ckv_refs, kr_refs = refs[:pages], refs[pages:2 * pages]
    (wukt_ref, wuktp_ref, wuv_ref, gains_ref, o_ref,
     wext_ref, q16_ref, qr_ref, m_ref, l_ref, u_ref) = refs[2 * pages:]
    j = pl.program_id(1)
    z8 = jnp.zeros((HEADS, LANE), F32)

    @pl.when(j == 0)
    def _():
        q16 = jnp.concatenate([qm_ref[...].astype(F32), z8], axis=0)
        q16_ref[...] = q16
        qr_ref[...] = q16[:, NOPE:NOPE + ROPE].astype(BF16)
        qg = q16 * gains_ref[3:4, :]
        row = lax.broadcasted_iota(jnp.int32, (ROWS, MLA_WIDTH), 0)
        lane = lax.broadcasted_iota(jnp.int32, (ROWS, MLA_WIDTH), 1)
        qbd = jnp.where((lane // HEAD_SLAB) == row, jnp.concatenate([qg] * HEADS, axis=1), 0.0)
        qabs = _dot(qbd.astype(BF16), wuktp_ref[...])
        wext_ref[0:SB_WIDTH, :] = wukt_ref[...]
        wext_ref[SB_WIDTH:SB_WIDTH + ROWS, :] = qabs.astype(BF16)
        m_ref[...] = jnp.full_like(m_ref, NEG_INF)
        l_ref[...] = jnp.zeros_like(l_ref)
        u_ref[...] = jnp.zeros_like(u_ref)

    wext = wext_ref[...]
    qr = qr_ref[...]
    m, l, u = m_ref[...], l_ref[...], u_ref[...]
    ones8 = jnp.ones((HEADS, 2 * PAGE), F32)

    def pair(t):
        return jnp.concatenate([ckv_refs[2 * t][...], ckv_refs[2 * t + 1][...]], axis=0).astype(BF16)

    scores = []
    for t in range(pages // 2):
        kt = _dot_nt(wext, pair(t))
        kv3 = kt[0:SB_WIDTH].reshape(HEADS, NOPE, 2 * PAGE)
        ssq = jnp.sum(kv3 * kv3, axis=1)
        r = jnp.concatenate([lax.rsqrt(ssq * (1.0 / NOPE) + EPS), ones8], axis=0)
        krt = jnp.concatenate([kr_refs[2 * t][...], kr_refs[2 * t + 1][...]], axis=1).astype(BF16)
        scores.append((kt[SB_WIDTH:SB_WIDTH + ROWS] * r + _dot(qr, krt)) * MLA_SCALE)
    s = jnp.concatenate(scores, axis=1)
    m_new = jnp.maximum(m, jnp.max(s, axis=-1, keepdims=True))
    alpha = jnp.exp(m - m_new)
    p = jnp.exp(s - m_new)
    l = alpha * l + jnp.sum(p, axis=-1, keepdims=True)
    u = alpha * u
    for t in range(pages // 2):
        u = u + _dot(p[:, 2 * PAGE * t:2 * PAGE * (t + 1)].astype(BF16), pair(t))
    m_ref[...], l_ref[...], u_ref[...] = m_new, l, u
    m = m_new

    @pl.when(j == pl.num_programs(1) - 1)
    def _():
        k16 = jnp.concatenate([kmn_ref[...].astype(F32), z8], axis=0)
        v16 = jnp.concatenate([vmn_ref[...].astype(F32), z8], axis=0)
        s_new = jnp.sum(q16_ref[...] * k16, axis=-1, keepdims=True) * MLA_SCALE
        m_f = jnp.maximum(m, s_new)
        alpha = jnp.exp(m - m_f)
        p_new = jnp.exp(s_new - m_f)
        l_f = alpha * l + p_new
        o_full = _dot((alpha * u / l_f).astype(BF16), wuv_ref[...])
        row = lax.broadcasted_iota(jnp.int32, (ROWS, SB_WIDTH), 0)
        lane = lax.broadcasted_iota(jnp.int32, (ROWS, SB_WIDTH), 1)
        new = jnp.concatenate([v16 * (p_new / l_f)] * (SB_WIDTH // LANE), axis=1)
        o_ref[...] = jnp.sum(jnp.where((lane // V_DIM) == row, o_full + new, 0.0), axis=0, keepdims=True).astype(BF16)


def _mla_dec(page_table, qm, kmn, vmn, cache_ckv, cache_kr, wukt, wuktp, wuv, gains, pages):
    nseq, npages = page_table.shape
    steps = npages // pages

    def page_spec(t, shape):
        return pl.BlockSpec((None, None) + shape, lambda b, j, pt: (0, pt[b, j * pages + t], 0, 0))

    hspec = pl.BlockSpec((None, HEADS, LANE), lambda b, j, pt: (b, 0, 0))
    const = lambda a: pl.BlockSpec(a.shape, lambda b, j, pt: (0,) * a.ndim)
    ospec = pl.BlockSpec((None, 1, SB_WIDTH), lambda b, j, pt: (b, 0, 0))
    in_specs = ([hspec] * 3 + [page_spec(t, (PAGE, KV_RANK)) for t in range(pages)]
                + [page_spec(t, (ROPE, PAGE)) for t in range(pages)]
                + [const(wukt), const(wuktp), const(wuv), const(gains)])
    grid_spec = pltpu.PrefetchScalarGridSpec(
        num_scalar_prefetch=1, grid=(nseq, steps), in_specs=in_specs, out_specs=ospec,
        scratch_shapes=[pltpu.VMEM((SB_WIDTH + ROWS, KV_RANK), BF16), pltpu.VMEM((ROWS, LANE), F32),
                        pltpu.VMEM((ROWS, ROPE), BF16), pltpu.VMEM((ROWS, 1), F32),
                        pltpu.VMEM((ROWS, 1), F32), pltpu.VMEM((ROWS, KV_RANK), F32)])
    return pl.pallas_call(
        functools.partial(_mla_dec_body, pages=pages), grid_spec=grid_spec,
        out_shape=jax.ShapeDtypeStruct((nseq, 1, SB_WIDTH), BF16), name="mla_dec",
        compiler_params=pltpu.CompilerParams(dimension_semantics=("arbitrary",) * 2, vmem_limit_bytes=VMEM_LIMIT),
    )(page_table, qm, kmn, vmn, *([cache_ckv] * pages), *([cache_kr] * pages), wukt, wuktp, wuv, gains)


def _ffn_body(x_ref, osb_ref, oml_ref, gsb_ref, gml_ref, wsbo_ref, wmlo_ref, wout_ref, fn_ref, wgu_ref, wdn_ref,
              y_ref):
    y = gsb_ref[...] * _dot(osb_ref[...], wsbo_ref[...]) + gml_ref[...] * _dot(oml_ref[...], wmlo_ref[...])
    x2 = x_ref[...] + _dot(y.astype(BF16), wout_ref[...])
    hb = _rms(x2, fn_ref[...]).astype(BF16)
    gate = _dot(hb, wgu_ref[:, 0:D_FF])
    up = _dot(hb, wgu_ref[:, D_FF:2 * D_FF])
    act = (jax.nn.silu(gate) * up).astype(BF16)
    y_ref[...] = x2 + _dot(act, wdn_ref[...])


def _ffn(x, osb, oml, gsb, gml, wsbo, wmlo, wout, fn, wgu, wdn, tm, name):
    n = x.shape[0]
    row = lambda w: pl.BlockSpec((tm, w), lambda i: (i, 0))
    const = lambda a: pl.BlockSpec(a.shape, lambda i: (0,) * a.ndim, pipeline_mode=pl.Buffered(1))
    in_specs = [row(D_MODEL), row(SB_WIDTH), row(SB_WIDTH), row(D_MODEL), row(D_MODEL),
                const(wsbo), const(wmlo), const(wout), const(fn), const(wgu), const(wdn)]
    return pl.pallas_call(
        _ffn_body, grid=(n // tm,), in_specs=in_specs, out_specs=row(D_MODEL),
        out_shape=jax.ShapeDtypeStruct((n, D_MODEL), F32), name=name,
        compiler_params=pltpu.CompilerParams(dimension_semantics=("arbitrary",), vmem_limit_bytes=VMEM_LIMIT),
    )(x, osb, oml, gsb, gml, wsbo, wmlo, wout, fn, wgu, wdn)


ROWS_T_BLOCK = 256


def _rows_t_body(meta_ref, real_ref, o_ref):
    k = pl.program_id(1)
    last = pl.num_programs(1) - 1
    seq, width = real_ref.shape
    tb = ROWS_T_BLOCK

    @pl.when(k == 0)
    def _():
        o_ref[...] = jnp.concatenate([meta_ref[...], real_ref[0:tb - N_META, :]], axis=0).T

    @pl.when(jnp.logical_and(k > 0, k < last))
    def _():
        start = pl.multiple_of(k * tb - N_META, 8)
        o_ref[...] = real_ref[pl.ds(start, tb), :].T

    @pl.when(k == last)
    def _():
        tail = jnp.concatenate([real_ref[seq - N_META:seq, :], jnp.zeros((tb - N_META, width), F32)], axis=0)
        o_ref[...] = tail.T


def _rows_token_minor(meta, real):
    b, seq, width = real.shape
    assert seq % ROWS_T_BLOCK == 0 and meta.shape == (N_META, width)
    total = N_META + seq
    return pl.pallas_call(
        _rows_t_body, grid=(b, pl.cdiv(total, ROWS_T_BLOCK)),
        in_specs=[pl.BlockSpec((N_META, width), lambda b_, k: (0, 0)),
                  pl.BlockSpec((None, seq, width), lambda b_, k: (b_, 0, 0))],
        out_specs=pl.BlockSpec((None, width, ROWS_T_BLOCK), lambda b_, k: (b_, 0, k)),
        out_shape=jax.ShapeDtypeStruct((b, width, total), F32), name="rows_t",
        compiler_params=pltpu.CompilerParams(dimension_semantics=("arbitrary",) * 2, vmem_limit_bytes=VMEM_LIMIT),
    )(meta, real)


PROJ_TILE = 256
ATTN_TILE = 256
MLA_HEADS_PER_STEP = 8
FFN_TILE = 256
SB_DEC_PAGES = 2
MLA_DEC_PAGES = 16


def _pad_rows(a, rows):
    return jnp.concatenate([a, jnp.zeros((rows - a.shape[0],) + a.shape[1:], a.dtype)], axis=0)


def kernel(x_prompt, x_sample, cache_sb_k, cache_sb_v, cache_mla_ckv, cache_mla_krope, page_table, meta_tokens, attn_norm, w_in, q_a_norm, w_uq, q_nope_norm, q_rope_norm, kv_a_norm, k_rope_norm, w_ukv, k_nope_norm, w_sb_o, w_mla_o, w_out, ffn_norm, w_gu, w_down):
    depth = w_in.shape[0]
    assert depth == 1, "one layer per step"
    b, seq, d = x_prompt.shape
    nseq = x_sample.shape[0]
    npool = cache_sb_k.shape[1]
    past = page_table.shape[1] * PAGE

    win = _prep_w_in(w_in[0])
    wuq = _prep_w_uq(w_uq[0])
    wukv, wukt, wuktp, wuv = _prep_w_ukv(w_ukv[0])
    seg, w2 = _seg_matrix(), _cumsum_matrix()
    gains = _gain_rows(q_nope_norm[0], q_rope_norm[0], k_rope_norm[0], k_nope_norm[0])
    row = lambda a: a.reshape(1, -1)
    an, qan, kvn, fn = row(attn_norm[0]), row(q_a_norm[0]), row(kv_a_norm[0]), row(ffn_norm[0])
    wsbo, wmlo, wout = w_sb_o[0].astype(BF16), w_mla_o[0].astype(BF16), w_out[0].astype(BF16)
    wgu, wdn = w_gu[0].astype(BF16), w_down[0].astype(BF16)
    proj_w = (an, win, qan, wuq, seg, gains, kvn, wukv)

    tab_meta = _rope_tables(jnp.arange(N_META))
    tab_real = _rope_tables(N_META + jnp.arange(seq))
    tab_samp = _rope_tables(jnp.full((nseq,), past))

    xr = x_prompt.reshape(b * seq, d)
    (sk, sv, ckv, kr, sqb, skb, svb, qm, km, vm, gsb, gml) = _proj(
        xr, tab_real, seq // PROJ_TILE, PROJ_TILE, *proj_w, name="proj_prompt")
    (sk_m, sv_m, ckv_m, kr_m, _, skb_m, svb_m, _, km_m, vm_m, _, _) = _proj(
        meta_tokens.astype(F32), tab_meta, 1, N_META, *proj_w, name="proj_meta")
    r3 = lambda a: a.reshape(b, seq, a.shape[-1])
    o_sb = _sb_attn(r3(sqb), r3(skb), r3(svb), _pad_rows(skb_m, LANE), _pad_rows(svb_m, LANE), w2, ATTN_TILE)
    o_ml = _mla_attn(r3(qm), r3(km), r3(vm), _pad_rows(km_m, LANE), _pad_rows(vm_m, LANE), ATTN_TILE,
                     MLA_HEADS_PER_STEP)
    ffn_w = (wsbo, wmlo, wout, fn, wgu, wdn)
    y_prompt = _ffn(xr, o_sb.reshape(b * seq, -1), o_ml.reshape(b * seq, -1), gsb, gml, *ffn_w,
                    tm=FFN_TILE, name="ffn_prompt").reshape(b, seq, d)

    xs = x_sample.reshape(nseq, d)
    (sk_s, sv_s, ckv_s, kr_s, sqb_s, _, _, qm_s, km_s, vm_s, gsb_s, gml_s) = _proj(
        xs, tab_samp, 1, nseq, *proj_w, name="proj_sample")
    page_t = lambda c: jnp.transpose(c, (0, 1, 3, 4, 2)).reshape(depth, npool, SB_WIDTH, PAGE)
    o_sb_s = _sb_dec(page_table, sqb_s.reshape(nseq, 1, SB_WIDTH), page_t(cache_sb_k), page_t(cache_sb_v),
                     w2, SB_DEC_PAGES)
    h3 = lambda a: a.reshape(nseq, HEADS, LANE)
    o_ml_s = _mla_dec(page_table, h3(qm_s), h3(km_s), h3(vm_s), cache_mla_ckv,
                      jnp.transpose(cache_mla_krope, (0, 1, 3, 2)), wukt, wuktp, wuv, gains, MLA_DEC_PAGES)
    y_sample = _ffn(xs, o_sb_s.reshape(nseq, -1), o_ml_s.reshape(nseq, -1), gsb_s, gml_s, *ffn_w,
                    tm=nseq, name="ffn_sample").reshape(nseq, 1, d)

    def prompt_rows(meta, real, shape):
        m = jnp.broadcast_to(meta[None], (b,) + meta.shape)
        return jnp.concatenate([m, real.reshape(b, seq, -1)], axis=1).reshape((1, b, N_META + seq) + shape)

    def head_rows(meta, real):
        t = _rows_token_minor(meta, real.reshape(b, seq, SB_WIDTH)).reshape(1, b, HEADS, HEAD_DIM, N_META + seq)
        return jnp.transpose(t, (0, 1, 4, 2, 3))

    hd = (HEADS, HEAD_DIM)
    return (y_prompt, y_sample,
            head_rows(sk_m, sk), head_rows(sv_m, sv),
            prompt_rows(ckv_m, ckv, (KV_RANK,)), prompt_rows(kr_m, kr, (ROPE,)),
            sk_s.reshape((1, nseq, 1) + hd), sv_s.reshape((1, nseq, 1) + hd),
            ckv_s.reshape(1, nseq, 1, KV_RANK), kr_s.reshape(1, nseq, 1, ROPE))
```

```python
import functools

import numpy as np
import jax
import jax.numpy as jnp
from jax import lax
from jax.experimental import pallas as pl
from jax.experimental.pallas import tpu as pltpu

F32 = jnp.float32
BF16 = jnp.bfloat16

D_MODEL = 1024
N_META = 16
EPS = 1e-6
NEG_INF = -1e30
HEADS = 8
HEAD_DIM = 64
SB_WIDTH = HEADS * HEAD_DIM
SB_SCALE = HEAD_DIM ** -0.5
Q_RANK = 384
KV_RANK = 256
NOPE = 64
ROPE = 32
V_DIM = 64
MLA_SCALE = (NOPE + ROPE) ** -0.5
MLA_EXP2_SCALE = MLA_SCALE * float(np.log2(np.e))
ROPE_THETA = 10000.0
D_FF = 2816
PAGE = 128
IN_SPLITS = (SB_WIDTH, SB_WIDTH, SB_WIDTH, Q_RANK, KV_RANK, ROPE, D_MODEL, D_MODEL)
SPLIT_POINTS = tuple(int(v) for v in np.cumsum(IN_SPLITS)[:-1])

LANE = 128
HEAD_SLAB = LANE
MLA_WIDTH = HEADS * HEAD_SLAB
C_SQ, C_SK, C_SV = 0, 512, 1024
C_DQ, C_DKV = 1536, 1920
C_GSB, C_GML, C_KR = 2176, 3200, 4224
IN_WIDTH_P = 4352
VMEM_LIMIT = 56 * 1024 * 1024

NT_DIMS = (((1,), (1,)), ((), ()))


def _dot(a, b):
    return jnp.dot(a, b, preferred_element_type=F32)


def _dot_nt(a, b):
    return lax.dot_general(a, b, NT_DIMS, preferred_element_type=F32)


def _rms(x, g):
    return x * lax.rsqrt(jnp.mean(x * x, axis=-1, keepdims=True) + EPS) * g


def _prep_w_in(w):
    sq, sk, sv, dq, dkv, kr, gsb, gml = jnp.split(w, SPLIT_POINTS, axis=1)
    krg = jnp.concatenate([kr, jnp.roll(kr, 16, axis=1), jnp.zeros((D_MODEL, 64), F32)], axis=1)
    return jnp.concatenate([sq * SB_SCALE, sk, sv, dq, dkv, gsb, gml, krg], axis=1).astype(BF16)


def _prep_w_uq(w):
    w = w.reshape(Q_RANK, HEADS, NOPE + ROPE)
    nope, rope = w[..., :NOPE], w[..., NOPE:]
    z32 = jnp.zeros((Q_RANK, HEADS, 32), F32)
    z64 = jnp.zeros((Q_RANK, HEADS, 64), F32)
    main = jnp.concatenate([nope, rope, z32], axis=-1).reshape(Q_RANK, MLA_WIDTH)
    swap = jnp.concatenate([z64, jnp.roll(rope, 16, axis=-1), z32], axis=-1).reshape(Q_RANK, MLA_WIDTH)
    return jnp.concatenate([main, swap], axis=1).astype(BF16)


def _prep_w_ukv(w):
    w = w.reshape(KV_RANK, HEADS, NOPE + V_DIM)
    wk, wv = w[..., :NOPE], w[..., NOPE:]
    z64 = jnp.zeros((KV_RANK, HEADS, 64), F32)
    k_lay = jnp.concatenate([wk, z64], axis=-1).reshape(KV_RANK, MLA_WIDTH)
    lo = jnp.concatenate([wv, z64], axis=-1)
    hi = jnp.concatenate([z64, wv], axis=-1)
    odd = (jnp.arange(HEADS) % 2 == 1)[None, :, None]
    v_lay = jnp.where(odd, hi, lo).reshape(KV_RANK, MLA_WIDTH)
    w_proj = jnp.concatenate([k_lay, v_lay], axis=1).astype(BF16)
    wuk_t = wk.reshape(KV_RANK, HEADS * NOPE).T.astype(BF16)
    wuk_t_pad = k_lay.T.astype(BF16)
    wuv = wv.reshape(KV_RANK, HEADS * V_DIM).astype(BF16)
    return w_proj, wuk_t, wuk_t_pad, wuv


def _seg_matrix():
    m = np.zeros((LANE, LANE), np.float32)
    m[:NOPE, :NOPE] = 1.0 / NOPE
    m[NOPE:NOPE + ROPE, NOPE:NOPE + ROPE] = 1.0 / ROPE
    z = np.zeros_like(m)
    return jnp.asarray(np.block([[m, z], [z, m]]), BF16)


def _cumsum_matrix():
    j = np.arange(LANE)
    later = (j[:, None] > j[None, :]).astype(np.float32)
    half = np.concatenate([later, np.ones((LANE, LANE), np.float32)], axis=1)
    return jnp.asarray(np.concatenate([half, half], axis=0), BF16)


def _rope_tables(pos):
    half = ROPE // 2
    inv = ROPE_THETA ** (-jnp.arange(half, dtype=F32) / half)
    ang = pos.astype(F32)[:, None] * inv[None, :]
    cos, sin = jnp.cos(ang), jnp.sin(ang)
    t = pos.shape[0]
    cos2 = jnp.concatenate([cos, cos], axis=1)
    sin2 = jnp.concatenate([-sin, sin], axis=1)
    one64, z64, z32 = jnp.ones((t, 64), F32), jnp.zeros((t, 64), F32), jnp.zeros((t, 32), F32)
    qa = jnp.concatenate([one64, cos2, z32], axis=1)
    qb = jnp.concatenate([z64, sin2, z32], axis=1)
    kt = jnp.concatenate([cos2, sin2, z64], axis=1)
    return jnp.concatenate([qa, qb, kt], axis=1)


def _gain_rows(q_nope_norm, q_rope_norm, k_rope_norm, k_nope_norm):
    z32, z64 = jnp.zeros((32,), F32), jnp.zeros((64,), F32)
    rows = [
        jnp.concatenate([q_nope_norm, q_rope_norm, z32]),
        jnp.concatenate([z64, jnp.roll(q_rope_norm, 16), z32]),
        jnp.concatenate([k_rope_norm, jnp.roll(k_rope_norm, 16), z64]),
        jnp.concatenate([k_nope_norm, z64]),
    ]
    rows += [jnp.zeros((LANE,), F32)] * 4
    return jnp.stack(rows)


def _proj_body(x_ref, tab_ref, an_ref, win_ref, qan_ref, wuq_ref, seg_ref, gains_ref, kvn_ref, wukv_ref,
               sk_o, sv_o, ckv_o, kr_o, sqb_o, skb_o, svb_o, qm_o, km_o, vm_o, gsb_o, gml_o):
    hb = _rms(x_ref[...], an_ref[...]).astype(BF16)

    def cols(a, b):
        return _dot(hb, win_ref[:, a:b])

    sqb_o[...] = cols(C_SQ, C_SK).astype(BF16)
    sk = cols(C_SK, C_SV)
    sk_o[...] = sk
    skb_o[...] = sk.astype(BF16)
    sv = cols(C_SV, C_DQ)
    sv_o[...] = sv
    svb_o[...] = sv.astype(BF16)
    gsb_o[...] = jax.nn.sigmoid(cols(C_GSB, C_GML))
    gml_o[...] = jax.nn.sigmoid(cols(C_GML, C_KR))

    tab = tab_ref[...]
    gains = gains_ref[...]
    qa = tab[:, 0:LANE] * gains[0:1, :]
    qb = tab[:, LANE:2 * LANE] * gains[1:2, :]
    kt = tab[:, 2 * LANE:3 * LANE] * gains[2:3, :]
    gkn = gains[3:4, :]
    seg = seg_ref[...]

    dqn = _rms(cols(C_DQ, C_DKV), qan_ref[...]).astype(BF16)
    for s in range(MLA_WIDTH // 256):
        q = _dot(dqn, wuq_ref[:, s * 256:(s + 1) * 256])
        qs = _dot(dqn, wuq_ref[:, MLA_WIDTH + s * 256:MLA_WIDTH + (s + 1) * 256])
        r = lax.rsqrt(_dot((q * q).astype(BF16), seg) + EPS)
        qa2 = jnp.concatenate([qa, qa], axis=1)
        qb2 = jnp.concatenate([qb, qb], axis=1)
        qm_o[:, s * 256:(s + 1) * 256] = (r * (q * qa2 + qs * qb2)).astype(BF16)

    krg = cols(C_KR, IN_WIDTH_P)
    lane = lax.broadcasted_iota(jnp.int32, krg.shape, 1)
    ms = jnp.sum(jnp.where(lane < ROPE, krg * krg, 0.0), axis=-1, keepdims=True) * (1.0 / ROPE)
    t = krg * lax.rsqrt(ms + EPS) * kt
    kr_lo = jnp.where(lane < ROPE, t + pltpu.roll(t, LANE - ROPE, 1), 0.0)
    kr_o[...] = kr_lo[:, 0:ROPE]
    kr_slab = pltpu.roll(kr_lo, NOPE, 1)

    ckv = _rms(cols(C_DKV, C_GSB), kvn_ref[...])
    ckv_o[...] = ckv
    ckvb = ckv.astype(BF16)
    gkn2 = jnp.concatenate([gkn, gkn], axis=1)
    kr2 = jnp.concatenate([kr_slab, kr_slab], axis=1)
    lane2 = lax.broadcasted_iota(jnp.int32, (krg.shape[0], 2 * LANE), 1)
    free_half = jnp.logical_and(lane2 >= V_DIM, lane2 < 2 * LANE - V_DIM)
    for s in range(MLA_WIDTH // 256):
        k = _dot(ckvb, wukv_ref[:, s * 256:(s + 1) * 256])
        r = lax.rsqrt(_dot((k * k).astype(BF16), seg) + EPS)
        km_o[:, s * 256:(s + 1) * 256] = (k * r * gkn2 + kr2).astype(BF16)
        v = _dot(ckvb, wukv_ref[:, MLA_WIDTH + s * 256:MLA_WIDTH + (s + 1) * 256])
        vm_o[:, s * 256:(s + 1) * 256] = jnp.where(free_half, 1.0, v).astype(BF16)


def _proj(x, tab, tab_blocks, tm, an, win, qan, wuq, seg, gains, kvn, wukv, name):
    n = x.shape[0]
    grid = (n // tm,)
    tok = lambda w, dt: jax.ShapeDtypeStruct((n, w), dt)
    out_shape = (tok(512, F32), tok(512, F32), tok(KV_RANK, F32), tok(ROPE, F32),
                 tok(512, BF16), tok(512, BF16), tok(512, BF16),
                 tok(MLA_WIDTH, BF16), tok(MLA_WIDTH, BF16), tok(MLA_WIDTH, BF16),
                 tok(D_MODEL, F32), tok(D_MODEL, F32))
    row = lambda w: pl.BlockSpec((tm, w), lambda i: (i, 0))
    const = lambda a: pl.BlockSpec(a.shape, lambda i: (0,) * a.ndim)
    in_specs = [row(D_MODEL), pl.BlockSpec((tm, 3 * LANE), lambda i: (i % tab_blocks, 0)),
                const(an), const(win), const(qan), const(wuq), const(seg), const(gains), const(kvn), const(wukv)]
    out_specs = tuple(row(s.shape[1]) for s in out_shape)
    return pl.pallas_call(
        _proj_body, grid=grid, in_specs=in_specs, out_specs=out_specs, out_shape=out_shape, name=name,
        compiler_params=pltpu.CompilerParams(dimension_semantics=("arbitrary",), vmem_limit_bytes=VMEM_LIMIT),
    )(x, tab, an, win, qan, wuq, seg, gains, kvn, wukv)


SB_EXIT = -110.0


def _sb_logs(z):
    soft = jnp.log(1.0 + jnp.exp(-jnp.abs(z)))
    return jnp.minimum(-z, 0.0) - soft, jnp.minimum(z, 0.0) - soft


def _sb_block(lr, lb, c, w2, valid):
    tk = lr.shape[1]
    ws = [None] * (tk // LANE)
    for s in reversed(range(tk // LANE)):
        x = lr[:, s * LANE:(s + 1) * LANE]
        hi = x.astype(BF16)
        lo = (x - hi.astype(F32)).astype(BF16)
        r = _dot(jnp.concatenate([hi, lo], axis=1), w2)
        w = jnp.exp(lb[:, s * LANE:(s + 1) * LANE] + r[:, :LANE] + c)
        if valid is not None:
            w = jnp.where(valid[:, s * LANE:(s + 1) * LANE], w, 0.0)
        ws[s] = w.astype(BF16)
        c = c + r[:, LANE:]
    return jnp.concatenate(ws, axis=1), c


def _sb_attn_body(q_ref, k_ref, v_ref, km_ref, vm_ref, w2_ref, o_ref, qs_ref, acc_ref, c_ref, *, tq):
    i = pl.program_id(2)
    tk = tq
    lane_q = lax.broadcasted_iota(jnp.int32, (tq, LANE), 1)
    q = q_ref[...]
    zq = jnp.zeros_like(q)
    qs_ref[0:tq, :] = jnp.where(lane_q < HEAD_DIM, q, zq)
    qs_ref[tq:2 * tq, :] = jnp.where(lane_q >= HEAD_DIM, q, zq)
    acc_ref[...] = jnp.zeros_like(acc_ref)
    c_ref[...] = jnp.zeros_like(c_ref)
    w2 = w2_ref[...]

    def process(kb, vb, valid):
        z = _dot_nt(qs_ref[...], kb)
        lr, lb = _sb_logs(z)
        if valid is not None:
            lr = jnp.where(valid, lr, 0.0)
        w, c = _sb_block(lr, lb, c_ref[...], w2, valid)
        c_ref[...] = c
        nk = kb.shape[0]
        wcat = jnp.concatenate([w[:tq], w[tq:]], axis=1)
        lane_k = lax.broadcasted_iota(jnp.int32, (nk, LANE), 1)
        zv = jnp.zeros_like(vb)
        vstack = jnp.concatenate([jnp.where(lane_k < HEAD_DIM, vb, zv),
                                  jnp.where(lane_k >= HEAD_DIM, vb, zv)], axis=0)
        acc_ref[...] += _dot(wcat, vstack)

    row = lax.broadcasted_iota(jnp.int32, (2 * tq, tk), 0)
    col = lax.broadcasted_iota(jnp.int32, (2 * tq, tk), 1)
    row = jnp.where(row >= tq, row - tq, row)
    start = pl.multiple_of(i * tk, tk)
    process(k_ref[pl.ds(start, tk), :], v_ref[pl.ds(start, tk), :], col < row)

    def live():
        return jnp.max(c_ref[...]) > SB_EXIT

    def body(state):
        t, _ = state
        s = pl.multiple_of((i - 1 - t) * tk, tk)
        process(k_ref[pl.ds(s, tk), :], v_ref[pl.ds(s, tk), :], None)
        return t + 1, live()

    _, go = lax.while_loop(lambda st: jnp.logical_and(st[0] < i, st[1]), body, (jnp.int32(0), live()))

    @pl.when(go)
    def _():
        col_m = lax.broadcasted_iota(jnp.int32, (2 * tq, LANE), 1)
        process(km_ref[...], vm_ref[...], col_m < N_META)

    o_ref[...] = acc_ref[...].astype(BF16)


def _sb_attn(q, k, v, k_meta, v_meta, w2, tq):
    b, t, _ = q.shape
    grid = (b, SB_WIDTH // LANE, t // tq)
    qspec = pl.BlockSpec((None, tq, LANE), lambda b_, p, i: (b_, i, p))
    kspec = pl.BlockSpec((None, t, LANE), lambda b_, p, i: (b_, 0, p))
    mspec = pl.BlockSpec((LANE, LANE), lambda b_, p, i: (0, p))
    wspec = pl.BlockSpec(w2.shape, lambda b_, p, i: (0, 0))
    return pl.pallas_call(
        functools.partial(_sb_attn_body, tq=tq), grid=grid,
        in_specs=[qspec, kspec, kspec, mspec, mspec, wspec], out_specs=qspec,
        out_shape=jax.ShapeDtypeStruct(q.shape, BF16), name="sb_attn",
        scratch_shapes=[pltpu.VMEM((2 * tq, LANE), BF16), pltpu.VMEM((tq, LANE), F32),
                        pltpu.VMEM((2 * tq, LANE), F32)],
        compiler_params=pltpu.CompilerParams(dimension_semantics=("arbitrary",) * 3, vmem_limit_bytes=VMEM_LIMIT),
    )(q, k, v, k_meta, v_meta, w2)


def _mla_attn_body(q_ref, k_ref, v_ref, km_ref, vm_ref, o_ref, *, tq, nh):
    i = pl.program_id(2)
    tk = tq
    row = lax.broadcasted_iota(jnp.int32, (tq, tk), 0)
    col = lax.broadcasted_iota(jnp.int32, (tq, tk), 1)
    col_m = lax.broadcasted_iota(jnp.int32, (tq, LANE), 1)
    slabs = [slice(h * LANE, (h + 1) * LANE) for h in range(nh)]
    qs = [q_ref[:, ls] for ls in slabs]

    def scores(h, kb, valid):
        s = _dot_nt(qs[h], kb[:, slabs[h]])
        return s if valid is None else jnp.where(valid, s, NEG_INF)

    def sweep(fn, carry):
        carry = fn(km_ref, vm_ref, 0, LANE, col_m < N_META, carry)
        carry = lax.fori_loop(0, i, lambda j, c: fn(k_ref, v_ref, pl.multiple_of(j * tk, tk), tk, None, c), carry)
        return fn(k_ref, v_ref, pl.multiple_of(i * tk, tk), tk, col <= row, carry)

    def step(kr_, vr_, start, size, valid, carry):
        kb = kr_[pl.ds(start, size), :]
        vb = vr_[pl.ds(start, size), :]
        out = []
        for h in range(nh):
            m, a = carry[h]
            s = scores(h, kb, valid)
            m_new = jnp.maximum(m, jnp.max(s, axis=-1, keepdims=True))
            alpha = jnp.exp2((m - m_new) * MLA_EXP2_SCALE)
            p = jnp.exp2((s - m_new) * MLA_EXP2_SCALE)
            out.append((m_new, alpha * a + _dot(p.astype(BF16), vb[:, slabs[h]])))
        return tuple(out)

    init = (jnp.full((tq, 1), NEG_INF, F32), jnp.zeros((tq, LANE), F32))
    acc = [r[1] for r in sweep(step, (init,) * nh)]
    outs = [a / pltpu.roll(a, V_DIM, 1) for a in acc]
    pairs = [jnp.where(col_m < V_DIM, outs[h], outs[h + 1]) for h in range(0, nh, 2)]
    o_ref[...] = jnp.concatenate(pairs, axis=1).astype(BF16)


def _mla_attn(q, k, v, k_meta, v_meta, tq, nh):
    b, t, _ = q.shape
    grid = (b, HEADS // nh, t // tq)
    qspec = pl.BlockSpec((None, tq, nh * LANE), lambda b_, p, i: (b_, i, p))
    kspec = pl.BlockSpec((None, t, nh * LANE), lambda b_, p, i: (b_, 0, p))
    mspec = pl.BlockSpec((LANE, nh * LANE), lambda b_, p, i: (0, p))
    ospec = pl.BlockSpec((None, tq, nh * V_DIM), lambda b_, p, i: (b_, i, p))
    return pl.pallas_call(
        functools.partial(_mla_attn_body, tq=tq, nh=nh), grid=grid,
        in_specs=[qspec, kspec, kspec, mspec, mspec], out_specs=ospec,
        out_shape=jax.ShapeDtypeStruct((b, t, SB_WIDTH), BF16), name="mla_attn",
        compiler_params=pltpu.CompilerParams(dimension_semantics=("arbitrary",) * 3, vmem_limit_bytes=VMEM_LIMIT),
    )(q, k, v, k_meta, v_meta)


ROWS = 16


def _sb_dec_body(pt_ref, q_ref, *refs, pages):
    k_refs, v_refs = refs[:pages], refs[pages:2 * pages]
    k_hbm, v_hbm, w2_ref, o_ref, kbuf, vbuf, sem = refs[2 * pages:]
    b = pl.program_id(0)
    npages = pt_ref.shape[1]
    row = lax.broadcasted_iota(jnp.int32, (ROWS, SB_WIDTH), 0)
    lane = lax.broadcasted_iota(jnp.int32, (ROWS, SB_WIDTH), 1)
    diag = (lane // HEAD_DIM) == row
    head_row = lax.broadcasted_iota(jnp.int32, (ROWS, LANE), 0) < HEADS
    qbd = jnp.where(diag, jnp.broadcast_to(q_ref[...].astype(F32), (ROWS, SB_WIDTH)), 0.0).astype(BF16)
    w2 = w2_ref[...]

    def page(kt, vt, c, acc):
        z = _dot(qbd, kt.astype(BF16))
        lr, lb = _sb_logs(z)
        w, c = _sb_block(lr, lb, c, w2, None)
        return c, acc + _dot_nt(w, vt.astype(BF16))

    def live(c):
        return jnp.max(jnp.where(head_row, c, SB_EXIT)) > SB_EXIT

    c = jnp.zeros((ROWS, LANE), F32)
    acc = jnp.zeros((ROWS, SB_WIDTH), F32)
    for t in range(pages):
        c, acc = page(k_refs[t][...], v_refs[t][...], c, acc)

    def older(state):
        n, _, c, acc = state
        pg = pt_ref[b, npages - 1 - n]
        ck = pltpu.make_async_copy(k_hbm.at[0, pg], kbuf, sem.at[0])
        cv = pltpu.make_async_copy(v_hbm.at[0, pg], vbuf, sem.at[1])
        ck.start()
        cv.start()
        ck.wait()
        cv.wait()
        c, acc = page(kbuf[...], vbuf[...], c, acc)
        return n + 1, live(c), c, acc

    _, _, c, acc = lax.while_loop(lambda st: jnp.logical_and(st[0] < npages, st[1]), older,
                                  (jnp.int32(pages), live(c), c, acc))
    o_ref[...] = jnp.sum(jnp.where(diag, acc, 0.0), axis=0, keepdims=True).astype(BF16)


def _sb_dec(page_table, q, cache_k, cache_v, w2, pages):
    nseq, npages = page_table.shape

    def page_spec(t):
        return pl.BlockSpec((None, None, SB_WIDTH, PAGE), lambda b, pt: (0, pt[b, npages - 1 - t], 0, 0))

    qspec = pl.BlockSpec((None, 1, SB_WIDTH), lambda b, pt: (b, 0, 0))
    anyspec = pl.BlockSpec(memory_space=pl.ANY)
    in_specs = ([qspec] + [page_spec(t) for t in range(pages)] * 2
                + [anyspec, anyspec, pl.BlockSpec(w2.shape, lambda b, pt: (0, 0))])
    grid_spec = pltpu.PrefetchScalarGridSpec(
        num_scalar_prefetch=1, grid=(nseq,), in_specs=in_specs, out_specs=qspec,
        scratch_shapes=[pltpu.VMEM((SB_WIDTH, PAGE), F32), pltpu.VMEM((SB_WIDTH, PAGE), F32),
                        pltpu.SemaphoreType.DMA((2,))])
    return pl.pallas_call(
        functools.partial(_sb_dec_body, pages=pages), grid_spec=grid_spec,
        out_shape=jax.ShapeDtypeStruct((nseq, 1, SB_WIDTH), BF16), name="sb_dec",
        compiler_params=pltpu.CompilerParams(dimension_semantics=("arbitrary",), vmem_limit_bytes=VMEM_LIMIT),
    )(page_table, q, *([cache_k] * pages), *([cache_v] * pages), cache_k, cache_v, w2)


def _mla_dec_body(pt_ref, qm_ref, kmn_ref, vmn_ref, *refs, pages):
    ckv_refs, kr_refs = refs[:pages], refs[pages:2 * pages]
    (wukt_ref, wuktp_ref, wuv_ref, gains_ref, o_ref,
     wext_ref, q16_ref, qr_ref, m_ref, l_ref, u_ref) = refs[2 * pages:]
    j = pl.program_id(1)
    z8 = jnp.zeros((HEADS, LANE), F32)

    @pl.when(j == 0)
    def _():
        q16 = jnp.concatenate([qm_ref[...].astype(F32), z8], axis=0)
        q16_ref[...] = q16
        qr_ref[...] = q16[:, NOPE:NOPE + ROPE].astype(BF16)
        qg = q16 * gains_ref[3:4, :]
        row = lax.broadcasted_iota(jnp.int32, (ROWS, MLA_WIDTH), 0)
        lane = lax.broadcasted_iota(jnp.int32, (ROWS, MLA_WIDTH), 1)
        qbd = jnp.where((lane // HEAD_SLAB) == row, jnp.concatenate([qg] * HEADS, axis=1), 0.0)
        qabs = _dot(qbd.astype(BF16), wuktp_ref[...])
        wext_ref[0:SB_WIDTH, :] = wukt_ref[...]
        wext_ref[SB_WIDTH:SB_WIDTH + ROWS, :] = qabs.astype(BF16)
        m_ref[...] = jnp.full_like(m_ref, NEG_INF)
        l_ref[...] = jnp.zeros_like(l_ref)
        u_ref[...] = jnp.zeros_like(u_ref)

    wext = wext_ref[...]
    qr = qr_ref[...]
    m, l, u = m_ref[...], l_ref[...], u_ref[...]
    ones8 = jnp.ones((HEADS, 2 * PAGE), F32)

    def pair(t):
        return jnp.concatenate([ckv_refs[2 * t][...], ckv_refs[2 * t + 1][...]], axis=0).astype(BF16)

    scores = []
    for t in range(pages // 2):
        kt = _dot_nt(wext, pair(t))
        kv3 = kt[0:SB_WIDTH].reshape(HEADS, NOPE, 2 * PAGE)
        ssq = jnp.sum(kv3 * kv3, axis=1)
        r = jnp.concatenate([lax.rsqrt(ssq * (1.0 / NOPE) + EPS), ones8], axis=0)
        krt = jnp.concatenate([kr_refs[2 * t][...], kr_refs[2 * t + 1][...]], axis=1).astype(BF16)
        scores.append((kt[SB_WIDTH:SB_WIDTH + ROWS] * r + _dot(qr, krt)) * MLA_SCALE)
    s = jnp.concatenate(scores, axis=1)
    m_new = jnp.maximum(m, jnp.max(s, axis=-1, keepdims=True))
    alpha = jnp.exp(m - m_new)
    p = jnp.exp(s - m_new)
    l = alpha * l + jnp.sum(p, axis=-1, keepdims=True)
    u = alpha * u
    for t in range(pages // 2):
        u = u + _dot(p[:, 2 * PAGE * t:2 * PAGE * (t + 1)].astype(BF16), pair(t))
    m_ref[...], l_ref[...], u_ref[...] = m_new, l, u
    m = m_new

    @pl.when(j == pl.num_programs(1) - 1)
    def _():
        k16 = jnp.concatenate([kmn_ref[...].astype(F32), z8], axis=0)
        v16 = jnp.concatenate([vmn_ref[...].astype(F32), z8], axis=0)
        s_new = jnp.sum(q16_ref[...] * k16, axis=-1, keepdims=True) * MLA_SCALE
        m_f = jnp.maximum(m, s_new)
        alpha = jnp.exp(m - m_f)
        p_new = jnp.exp(s_new - m_f)
        l_f = alpha * l + p_new
        o_full = _dot((alpha * u / l_f).astype(BF16), wuv_ref[...])
        row = lax.broadcasted_iota(jnp.int32, (ROWS, SB_WIDTH), 0)
        lane = lax.broadcasted_iota(jnp.int32, (ROWS, SB_WIDTH), 1)
        new = jnp.concatenate([v16 * (p_new / l_f)] * (SB_WIDTH // LANE), axis=1)
        o_ref[...] = jnp.sum(jnp.where((lane // V_DIM) == row, o_full + new, 0.0), axis=0, keepdims=True).astype(BF16)


def _mla_dec(page_table, qm, kmn, vmn, cache_ckv, cache_kr, wukt, wuktp, wuv, gains, pages):
    nseq, npages = page_table.shape
    steps = npages // pages

    def page_spec(t, shape):
        return pl.BlockSpec((None, None) + shape, lambda b, j, pt: (0, pt[b, j * pages + t], 0, 0))

    hspec = pl.BlockSpec((None, HEADS, LANE), lambda b, j, pt: (b, 0, 0))
    const = lambda a: pl.BlockSpec(a.shape, lambda b, j, pt: (0,) * a.ndim)
    ospec = pl.BlockSpec((None, 1, SB_WIDTH), lambda b, j, pt: (b, 0, 0))
    in_specs = ([hspec] * 3 + [page_spec(t, (PAGE, KV_RANK)) for t in range(pages)]
                + [page_spec(t, (ROPE, PAGE)) for t in range(pages)]
                + [const(wukt), const(wuktp), const(wuv), const(gains)])
    grid_spec = pltpu.PrefetchScalarGridSpec(
        num_scalar_prefetch=1, grid=(nseq, steps), in_specs=in_specs, out_specs=ospec,
        scratch_shapes=[pltpu.VMEM((SB_WIDTH + ROWS, KV_RANK), BF16), pltpu.VMEM((ROWS, LANE), F32),
                        pltpu.VMEM((ROWS, ROPE), BF16), pltpu.VMEM((ROWS, 1), F32),
                        pltpu.VMEM((ROWS, 1), F32), pltpu.VMEM((ROWS, KV_RANK), F32)])
    return pl.pallas_call(
        functools.partial(_mla_dec_body, pages=pages), grid_spec=grid_spec,
        out_shape=jax.ShapeDtypeStruct((nseq, 1, SB_WIDTH), BF16), name="mla_dec",
        compiler_params=pltpu.CompilerParams(dimension_semantics=("arbitrary",) * 2, vmem_limit_bytes=VMEM_LIMIT),
    )(page_table, qm, kmn, vmn, *([cache_ckv] * pages), *([cache_kr] * pages), wukt, wuktp, wuv, gains)


def _ffn_body(x_ref, osb_ref, oml_ref, gsb_ref, gml_ref, wsbo_ref, wmlo_ref, wout_ref, fn_ref, wgu_ref, wdn_ref,
              y_ref):
    y = gsb_ref[...] * _dot(osb_ref[...], wsbo_ref[...]) + gml_ref[...] * _dot(oml_ref[...], wmlo_ref[...])
    x2 = x_ref[...] + _dot(y.astype(BF16), wout_ref[...])
    hb = _rms(x2, fn_ref[...]).astype(BF16)
    gate = _dot(hb, wgu_ref[:, 0:D_FF])
    up = _dot(hb, wgu_ref[:, D_FF:2 * D_FF])
    act = (jax.nn.silu(gate) * up).astype(BF16)
    y_ref[...] = x2 + _dot(act, wdn_ref[...])


def _ffn(x, osb, oml, gsb, gml, wsbo, wmlo, wout, fn, wgu, wdn, tm, name):
    n = x.shape[0]
    row = lambda w: pl.BlockSpec((tm, w), lambda i: (i, 0))
    const = lambda a: pl.BlockSpec(a.shape, lambda i: (0,) * a.ndim, pipeline_mode=pl.Buffered(1))
    in_specs = [row(D_MODEL), row(SB_WIDTH), row(SB_WIDTH), row(D_MODEL), row(D_MODEL),
                const(wsbo), const(wmlo), const(wout), const(fn), const(wgu), const(wdn)]
    return pl.pallas_call(
        _ffn_body, grid=(n // tm,), in_specs=in_specs, out_specs=row(D_MODEL),
        out_shape=jax.ShapeDtypeStruct((n, D_MODEL), F32), name=name,
        compiler_params=pltpu.CompilerParams(dimension_semantics=("arbitrary",), vmem_limit_bytes=VMEM_LIMIT),
    )(x, osb, oml, gsb, gml, wsbo, wmlo, wout, fn, wgu, wdn)


ROWS_T_BLOCK = 256


def _rows_t_body(meta_ref, real_ref, o_ref):
    k = pl.program_id(1)
    last = pl.num_programs(1) - 1
    seq, width = real_ref.shape
    tb = ROWS_T_BLOCK

    @pl.when(k == 0)
    def _():
        o_ref[...] = jnp.concatenate([meta_ref[...], real_ref[0:tb - N_META, :]], axis=0).T

    @pl.when(jnp.logical_and(k > 0, k < last))
    def _():
        start = pl.multiple_of(k * tb - N_META, 8)
        o_ref[...] = real_ref[pl.ds(start, tb), :].T

    @pl.when(k == last)
    def _():
        tail = jnp.concatenate([real_ref[seq - N_META:seq, :], jnp.zeros((tb - N_META, width), F32)], axis=0)
        o_ref[...] = tail.T


def _rows_token_minor(meta, real):
    b, seq, width = real.shape
    assert seq % ROWS_T_BLOCK == 0 and meta.shape == (N_META, width)
    total = N_META + seq
    return pl.pallas_call(
        _rows_t_body, grid=(b, pl.cdiv(total, ROWS_T_BLOCK)),
        in_specs=[pl.BlockSpec((N_META, width), lambda b_, k: (0, 0)),
                  pl.BlockSpec((None, seq, width), lambda b_, k: (b_, 0, 0))],
        out_specs=pl.BlockSpec((None, width, ROWS_T_BLOCK), lambda b_, k: (b_, 0, k)),
        out_shape=jax.ShapeDtypeStruct((b, width, total), F32), name="rows_t",
        compiler_params=pltpu.CompilerParams(dimension_semantics=("arbitrary",) * 2, vmem_limit_bytes=VMEM_LIMIT),
    )(meta, real)


PROJ_TILE = 256
ATTN_TILE = 256
MLA_HEADS_PER_STEP = 8
FFN_TILE = 256
SB_DEC_PAGES = 2
MLA_DEC_PAGES = 32


def _pad_rows(a, rows):
    return jnp.concatenate([a, jnp.zeros((rows - a.shape[0],) + a.shape[1:], a.dtype)], axis=0)


def kernel(x_prompt, x_sample, cache_sb_k, cache_sb_v, cache_mla_ckv, cache_mla_krope, page_table, meta_tokens, attn_norm, w_in, q_a_norm, w_uq, q_nope_norm, q_rope_norm, kv_a_norm, k_rope_norm, w_ukv, k_nope_norm, w_sb_o, w_mla_o, w_out, ffn_norm, w_gu, w_down):
    depth = w_in.shape[0]
    assert depth == 1, "one layer per step"
    b, seq, d = x_prompt.shape
    nseq = x_sample.shape[0]
    npool = cache_sb_k.shape[1]
    past = page_table.shape[1] * PAGE

    win = _prep_w_in(w_in[0])
    wuq = _prep_w_uq(w_uq[0])
    wukv, wukt, wuktp, wuv = _prep_w_ukv(w_ukv[0])
    seg, w2 = _seg_matrix(), _cumsum_matrix()
    gains = _gain_rows(q_nope_norm[0], q_rope_norm[0], k_rope_norm[0], k_nope_norm[0])
    row = lambda a: a.reshape(1, -1)
    an, qan, kvn, fn = row(attn_norm[0]), row(q_a_norm[0]), row(kv_a_norm[0]), row(ffn_norm[0])
    wsbo, wmlo, wout = w_sb_o[0].astype(BF16), w_mla_o[0].astype(BF16), w_out[0].astype(BF16)
    wgu, wdn = w_gu[0].astype(BF16), w_down[0].astype(BF16)
    proj_w = (an, win, qan, wuq, seg, gains, kvn, wukv)

    tab_meta = _rope_tables(jnp.arange(N_META))
    tab_real = _rope_tables(N_META + jnp.arange(seq))
    tab_samp = _rope_tables(jnp.full((nseq,), past))

    xr = x_prompt.reshape(b * seq, d)
    (sk, sv, ckv, kr, sqb, skb, svb, qm, km, vm, gsb, gml) = _proj(
        xr, tab_real, seq // PROJ_TILE, PROJ_TILE, *proj_w, name="proj_prompt")
    (sk_m, sv_m, ckv_m, kr_m, _, skb_m, svb_m, _, km_m, vm_m, _, _) = _proj(
        meta_tokens.astype(F32), tab_meta, 1, N_META, *proj_w, name="proj_meta")
    r3 = lambda a: a.reshape(b, seq, a.shape[-1])
    o_sb = _sb_attn(r3(sqb), r3(skb), r3(svb), _pad_rows(skb_m, LANE), _pad_rows(svb_m, LANE), w2, ATTN_TILE)
    o_ml = _mla_attn(r3(qm), r3(km), r3(vm), _pad_rows(km_m, LANE), _pad_rows(vm_m, LANE), ATTN_TILE,
                     MLA_HEADS_PER_STEP)
    ffn_w = (wsbo, wmlo, wout, fn, wgu, wdn)
    y_prompt = _ffn(xr, o_sb.reshape(b * seq, -1), o_ml.reshape(b * seq, -1), gsb, gml, *ffn_w,
                    tm=FFN_TILE, name="ffn_prompt").reshape(b, seq, d)

    xs = x_sample.reshape(nseq, d)
    (sk_s, sv_s, ckv_s, kr_s, sqb_s, _, _, qm_s, km_s, vm_s, gsb_s, gml_s) = _proj(
        xs, tab_samp, 1, nseq, *proj_w, name="proj_sample")
    page_t = lambda c: jnp.transpose(c, (0, 1, 3, 4, 2)).reshape(depth, npool, SB_WIDTH, PAGE)
    o_sb_s = _sb_dec(page_table, sqb_s.reshape(nseq, 1, SB_WIDTH), page_t(cache_sb_k), page_t(cache_sb_v),
                     w2, SB_DEC_PAGES)
    h3 = lambda a: a.reshape(nseq, HEADS, LANE)
    o_ml_s = _mla_dec(page_table, h3(qm_s), h3(km_s), h3(vm_s), cache_mla_ckv,
                      jnp.transpose(cache_mla_krope, (0, 1, 3, 2)), wukt, wuktp, wuv, gains, MLA_DEC_PAGES)
    y_sample = _ffn(xs, o_sb_s.reshape(nseq, -1), o_ml_s.reshape(nseq, -1), gsb_s, gml_s, *ffn_w,
                    tm=nseq, name="ffn_sample").reshape(nseq, 1, d)

    def prompt_rows(meta, real, shape):
        m = jnp.broadcast_to(meta[None], (b,) + meta.shape)
        return jnp.concatenate([m, real.reshape(b, seq, -1)], axis=1).reshape((1, b, N_META + seq) + shape)

    def head_rows(meta, real):
        t = _rows_token_minor(meta, real.reshape(b, seq, SB_WIDTH)).reshape(1, b, HEADS, HEAD_DIM, N_META + seq)
        return jnp.transpose(t, (0, 1, 4, 2, 3))

    hd = (HEADS, HEAD_DIM)
    return (y_prompt, y_sample,
            head_rows(sk_m, sk), head_rows(sv_m, sv),
            prompt_rows(ckv_m, ckv, (KV_RANK,)), prompt_rows(kr_m, kr, (ROPE,)),
            sk_s.reshape((1, nseq, 1) + hd), sv_s.reshape((1, nseq, 1) + hd),
            ckv_s.reshape(1, nseq, 1, KV_RANK), kr_s.reshape(1, nseq, 1, ROPE))
```
